```python
import math
import jax, jax.numpy as jnp
from jax import lax
import numpy as np

D_MODEL = 1024
BATCH = 4
SEQ = 4096
DEPTH = 4

MIX_WIDTH = D_MODEL
LRU_WIDTH = D_MODEL // 2
LRU_BLOCKS = 8
LRU_BLOCK = LRU_WIDTH // LRU_BLOCKS
LRU_C = 8.0
GDN_HEAD_DIM = 128
GDN_HEADS = (MIX_WIDTH - LRU_WIDTH) // GDN_HEAD_DIM
GDN_WIDTH = GDN_HEADS * GDN_HEAD_DIM
GDN_CHUNK = 64
CONV_WIDTH = 4
D_FF = 4 * D_MODEL
N_MOD = 6
IN_COLS = 2 * LRU_WIDTH + 4 * GDN_WIDTH + 2 * GDN_HEADS
NORM_EPS = 1e-6

kernel_name = "hymba_rglru_gdn_hybrid"


def rms_norm(x, w):
    xf = x.astype(jnp.float32)
    y = xf * lax.rsqrt(jnp.mean(xf * xf, axis=-1, keepdims=True) + NORM_EPS)
    return (y * w.astype(jnp.float32)).astype(x.dtype)


def causal_depthwise_conv(x, w):
    K = w.shape[0]
    S = x.shape[1]
    xp = jnp.pad(x, ((0, 0), (K - 1, 0), (0, 0)))
    y = xp[:, 0:S] * w[0]
    for k in range(1, K):
        y = y + xp[:, k:k + S] * w[k]
    return y


def rg_lru(x, r_pre, i_pre, lam):
    dt = x.dtype
    xf = x.astype(jnp.float32)
    r = jax.nn.sigmoid(r_pre.astype(jnp.float32))
    i = jax.nn.sigmoid(i_pre.astype(jnp.float32))
    log_a = LRU_C * r * jax.nn.log_sigmoid(lam.astype(jnp.float32))
    a = jnp.exp(log_a)
    mult = jnp.sqrt(jnp.maximum(-jnp.expm1(2.0 * log_a), 1e-12))
    b = mult * (i * xf)

    def combine(left, right):
        a1, b1 = left
        a2, b2 = right
        return a1 * a2, a2 * b1 + b2

    _, h = lax.associative_scan(combine, (a, b), axis=1)
    return h.astype(dt)


def l2_normalize(t):
    return t * lax.rsqrt(jnp.sum(t * t, axis=-1, keepdims=True) + 1e-6)


def gated_delta_rule_chunked(q, k, v, g, beta):
    dt = v.dtype
    B, S, H, Dk = q.shape
    Dv = v.shape[-1]
    C = GDN_CHUNK
    N = S // C
    q = l2_normalize(q.astype(jnp.float32)) * (Dk ** -0.5)
    k = l2_normalize(k.astype(jnp.float32))
    v = v.astype(jnp.float32)

    def chunks(t):
        return t.reshape(B, N, C, H, -1).transpose(0, 3, 1, 2, 4)

    q, k, v = chunks(q), chunks(k), chunks(v)
    g = g.astype(jnp.float32).reshape(B, N, C, H).transpose(0, 3, 1, 2)
    beta = beta.astype(jnp.float32).reshape(B, N, C, H).transpose(0, 3, 1, 2)
    g = jnp.cumsum(g, axis=-1)

    causal = jnp.tril(jnp.ones((C, C), dtype=bool))
    strict = jnp.tril(jnp.ones((C, C), dtype=bool), k=-1)
    decay = jnp.exp(jnp.where(causal, g[..., :, None] - g[..., None, :], -jnp.inf))

    k_beta = k * beta[..., None]
    v_beta = v * beta[..., None]
    Lmat = jnp.where(strict, jnp.einsum('bhnid,bhnjd->bhnij', k_beta, k) * decay, 0.0)
    tmat = Lmat + jnp.eye(C, dtype=jnp.float32)
    u = lax.linalg.triangular_solve(tmat, v_beta, left_side=True, lower=True, unit_diagonal=True)
    w = lax.linalg.triangular_solve(tmat, k_beta * jnp.exp(g)[..., None],
                                    left_side=True, lower=True, unit_diagonal=True)
    attn = jnp.where(causal, jnp.einsum('bhnid,bhnjd->bhnij', q, k) * decay, 0.0)
    q_dec = q * jnp.exp(g)[..., None]
    k_tail = k * jnp.exp(g[..., -1:] - g)[..., None]
    g_last = jnp.exp(g[..., -1])

    def to_front(t):
        return jnp.moveaxis(t, 2, 0)

    xs = (to_front(u), to_front(w), to_front(attn), to_front(q_dec), to_front(k_tail),
          jnp.moveaxis(g_last, 2, 0))

    def step(state, inp):
        u_n, w_n, attn_n, qd_n, kt_n, gl_n = inp
        v_new = u_n - jnp.einsum('bhck,bhkv->bhcv', w_n, state)
        o = jnp.einsum('bhck,bhkv->bhcv', qd_n, state) + jnp.einsum('bhij,bhjv->bhiv', attn_n, v_new)
        state = state * gl_n[..., None, None] + jnp.einsum('bhck,bhcv->bhkv', kt_n, v_new)
        return state, o

    state0 = jnp.zeros((B, H, Dk, Dv), jnp.float32)
    _, o = lax.scan(step, state0, xs)
    o = o.transpose(1, 0, 3, 2, 4).reshape(B, S, H, Dv)
    return o.astype(dt)


def setup_inputs(seed: int = 0) -> dict:
    key = jax.random.key(seed)
    ks = jax.random.split(key, 24)
    f32 = jnp.float32
    L, D = DEPTH, D_MODEL

    def nrm(k, shape, std):
        return jax.random.normal(k, shape, f32) * std

    x = nrm(ks[0], (BATCH, SEQ, D), 1.0)
    c = nrm(ks[1], (BATCH, D), 1.0)
    norm_mix_w = 1.0 + nrm(ks[2], (L, D), 0.02)
    norm_mlp_w = 1.0 + nrm(ks[3], (L, D), 0.02)
    w_mod = nrm(ks[4], (L, D, N_MOD * D), 0.005)
    gate_offset = jnp.array([0.0, 0.0, 1.0, 0.0, 0.0, 1.0], f32)[None, :, None]
    b_mod = (nrm(ks[5], (L, N_MOD, D), 0.02) + gate_offset).reshape(L, N_MOD * D)
    w_in = nrm(ks[6], (L, D, IN_COLS), D ** -0.5)
    lru_conv_w = nrm(ks[7], (L, CONV_WIDTH, LRU_WIDTH), CONV_WIDTH ** -0.5)
    lru_conv_b = nrm(ks[8], (L, LRU_WIDTH), 0.01)
    lru_gate_a_w = nrm(ks[9], (L, LRU_BLOCKS, LRU_BLOCK, LRU_BLOCK), LRU_BLOCK ** -0.5)
    lru_gate_a_b = nrm(ks[10], (L, LRU_WIDTH), 0.01)
    lru_gate_x_w = nrm(ks[11], (L, LRU_BLOCKS, LRU_BLOCK, LRU_BLOCK), LRU_BLOCK ** -0.5)
    lru_gate_x_b = nrm(ks[12], (L, LRU_WIDTH), 0.01)
    u = jax.random.uniform(ks[13], (L, LRU_WIDTH), f32, 0.9, 0.999)
    p = u ** (1.0 / LRU_C)
    lru_lambda = jnp.log(p) - jnp.log1p(-p)
    lru_norm_w = 1.0 + nrm(ks[14], (L, LRU_WIDTH), 0.02)
    gdn_conv_w = nrm(ks[15], (L, CONV_WIDTH, 3 * GDN_WIDTH), CONV_WIDTH ** -0.5)
    gdn_a_log = jnp.log(jax.random.uniform(ks[16], (L, GDN_HEADS), f32, 1.0, 16.0))
    dt0 = jnp.exp(jax.random.uniform(ks[17], (L, GDN_HEADS), f32, math.log(1e-3), math.log(1e-1)))
    gdn_dt_bias = dt0 + jnp.log(-jnp.expm1(-dt0))
    gdn_norm_w = 1.0 + nrm(ks[18], (L, GDN_HEAD_DIM), 0.02)
    w_out = nrm(ks[19], (L, MIX_WIDTH, D), MIX_WIDTH ** -0.5)
    w_up = nrm(ks[20], (L, D, D_FF), D ** -0.5)
    w_down = nrm(ks[21], (L, D_FF, D), D_FF ** -0.5)
    final_norm_w = 1.0 + nrm(ks[22], (D,), 0.02)
    return {
        "x": x, "c": c,
        "norm_mix_w": norm_mix_w, "norm_mlp_w": norm_mlp_w,
        "w_mod": w_mod, "b_mod": b_mod,
        "w_in": w_in,
        "lru_conv_w": lru_conv_w, "lru_conv_b": lru_conv_b,
        "lru_gate_a_w": lru_gate_a_w, "lru_gate_a_b": lru_gate_a_b,
        "lru_gate_x_w": lru_gate_x_w, "lru_gate_x_b": lru_gate_x_b,
        "lru_lambda": lru_lambda, "lru_norm_w": lru_norm_w,
        "gdn_conv_w": gdn_conv_w, "gdn_a_log": gdn_a_log, "gdn_dt_bias": gdn_dt_bias,
        "gdn_norm_w": gdn_norm_w,
        "w_out": w_out, "w_up": w_up, "w_down": w_down,
        "final_norm_w": final_norm_w,
    }


def reference(x, c, norm_mix_w, norm_mlp_w, w_mod, b_mod, w_in,
              lru_conv_w, lru_conv_b, lru_gate_a_w, lru_gate_a_b,
              lru_gate_x_w, lru_gate_x_b, lru_lambda, lru_norm_w,
              gdn_conv_w, gdn_a_log, gdn_dt_bias, gdn_norm_w,
              w_out, w_up, w_down, final_norm_w):
    B, S, D = x.shape
    o_lx = 0
    o_ly = o_lx + LRU_WIDTH
    o_q = o_ly + LRU_WIDTH
    o_v_end = o_q + 3 * GDN_WIDTH
    o_z = o_v_end
    o_beta = o_z + GDN_WIDTH
    o_alpha = o_beta + GDN_HEADS
    c_act = jax.nn.silu(c)

    for l in range(DEPTH):
        mod = c_act @ w_mod[l] + b_mod[l]
        sh1, sc1, g1, sh2, sc2, g2 = jnp.split(mod[:, None, :], N_MOD, axis=-1)

        h = rms_norm(x, norm_mix_w[l]) * (1.0 + sc1) + sh1
        proj = h @ w_in[l]

        x_lru = proj[..., o_lx:o_ly]
        y_lru = proj[..., o_ly:o_q]
        xr = causal_depthwise_conv(x_lru, lru_conv_w[l]) + lru_conv_b[l]
        xb = xr.reshape(B, S, LRU_BLOCKS, LRU_BLOCK)
        r_pre = jnp.einsum('bsgi,gij->bsgj', xb, lru_gate_a_w[l]).reshape(B, S, LRU_WIDTH) + lru_gate_a_b[l]
        i_pre = jnp.einsum('bsgi,gij->bsgj', xb, lru_gate_x_w[l]).reshape(B, S, LRU_WIDTH) + lru_gate_x_b[l]
        h_lru = rg_lru(xr, r_pre, i_pre, lru_lambda[l])
        out_lru = rms_norm(h_lru * jax.nn.gelu(y_lru), lru_norm_w[l])

        qkv = jax.nn.silu(causal_depthwise_conv(proj[..., o_q:o_v_end], gdn_conv_w[l]))
        q, k, v = jnp.split(qkv.reshape(B, S, 3, GDN_HEADS, GDN_HEAD_DIM), 3, axis=2)
        q, k, v = q[:, :, 0], k[:, :, 0], v[:, :, 0]
        z = proj[..., o_z:o_beta].reshape(B, S, GDN_HEADS, GDN_HEAD_DIM)
        beta = jax.nn.sigmoid(proj[..., o_beta:o_alpha].astype(jnp.float32))
        g = -jnp.exp(gdn_a_log[l].astype(jnp.float32)) * jax.nn.softplus(
            proj[..., o_alpha:o_alpha + GDN_HEADS].astype(jnp.float32) + gdn_dt_bias[l].astype(jnp.float32))
        o = gated_delta_rule_chunked(q, k, v, g, beta)
        out_gdn = (rms_norm(o, gdn_norm_w[l]) * jax.nn.silu(z)).reshape(B, S, GDN_WIDTH)

        mix = jnp.concatenate([out_lru, out_gdn], axis=-1) @ w_out[l]
        x = x + g1 * mix

        h = rms_norm(x, norm_mlp_w[l]) * (1.0 + sc2) + sh2
        x = x + g2 * (jnp.square(jax.nn.relu(h @ w_up[l])) @ w_down[l])

    return rms_norm(x, final_norm_w)
```

```python
import functools
import math

import jax
import jax.numpy as jnp
from jax import lax
from jax.experimental import pallas as pl
from jax.experimental.pallas import tpu as pltpu

LRU_BLOCKS = 8
LRU_C = 8.0
GDN_HEAD_DIM = 128
GDN_CHUNK = 64
CONV_WIDTH = 4
N_MOD = 6
NORM_EPS = 1e-6

LANES = 128
SUBLANES = 8
MXU_DIM = 256
VMEM_LIMIT_BYTES = 56 * 1024 * 1024

F32 = jnp.float32
BF16 = jnp.bfloat16


def _compiler_params(semantics):
    return pltpu.CompilerParams(dimension_semantics=semantics,
                                vmem_limit_bytes=VMEM_LIMIT_BYTES)


def _sigmoid(x):
    return 1.0 / (1.0 + jnp.exp(-x))


def _silu(x):
    return x * _sigmoid(x)


def _softplus(x):
    return jnp.maximum(x, 0.0) + jnp.log1p(jnp.exp(-jnp.abs(x)))


def _rms_scale(x):
    return lax.rsqrt(jnp.mean(x * x, axis=-1, keepdims=True) + NORM_EPS)


def _mod_kernel(c_ref, w_ref, b_ref, o_ref):
    c_act = _silu(c_ref[...]).astype(BF16)
    o_ref[...] = jnp.dot(c_act, w_ref[...].astype(BF16),
                         preferred_element_type=F32) + b_ref[...]


def _modulation(c, w_mod, b_mod):
    depth, d, n = w_mod.shape
    b = c.shape[0]
    bp = max(SUBLANES, b)
    c_pad = jnp.zeros((bp, d), F32).at[:b].set(c)
    tn = n // N_MOD
    out = pl.pallas_call(
        _mod_kernel,
        grid=(depth, n // tn),
        in_specs=[
            pl.BlockSpec((bp, d), lambda l, j: (0, 0)),
            pl.BlockSpec((None, d, tn), lambda l, j: (l, 0, j)),
            pl.BlockSpec((None, 1, tn), lambda l, j: (l, 0, j)),
        ],
        out_specs=pl.BlockSpec((None, bp, tn), lambda l, j: (l, 0, j)),
        out_shape=jax.ShapeDtypeStruct((depth, bp, n), F32),
        compiler_params=_compiler_params(("arbitrary", "arbitrary")),
        name="modulation",
    )(c_pad, w_mod, b_mod.reshape(depth, 1, n))
    return out[:, :b].reshape(depth, b, N_MOD, 1, d)


def _inproj_kernel(x_ref, mod_ref, nw_ref, w_ref, gp_ref, o_ref, *, n_heads):
    x = x_ref[...]
    shift = mod_ref[0]
    scale = mod_ref[1]
    h = (x * _rms_scale(x) * nw_ref[...]) * (1.0 + scale) + shift
    proj = jnp.dot(h.astype(BF16), w_ref[...], preferred_element_type=F32)
    n = proj.shape[1]
    o_ref[:, : n - LANES] = proj[:, : n - LANES]
    tail = proj[:, n - LANES:]
    lane = lax.broadcasted_iota(jnp.int32, tail.shape, 1)
    beta = _sigmoid(tail)
    neg_a = gp_ref[0:1, :]
    dt_bias = gp_ref[1:2, :]
    g = neg_a * _softplus(tail + dt_bias)
    o_ref[:, n - LANES:] = jnp.where(lane < n_heads, beta,
                                     jnp.where(lane < 2 * n_heads, g, 0.0))


def _inproj(x2d, mod_l, norm_w, w_in_pad, gate_params, *, seq, tm, n_heads):
    t, d = x2d.shape
    n = w_in_pad.shape[1]
    tiles_per_seq = seq // tm
    return pl.pallas_call(
        functools.partial(_inproj_kernel, n_heads=n_heads),
        grid=(t // tm,),
        in_specs=[
            pl.BlockSpec((tm, d), lambda i: (i, 0)),
            pl.BlockSpec((None, N_MOD, 1, d), lambda i: (i // tiles_per_seq, 0, 0, 0)),
            pl.BlockSpec((1, d), lambda i: (0, 0)),
            pl.BlockSpec((d, n), lambda i: (0, 0)),
            pl.BlockSpec((SUBLANES, LANES), lambda i: (0, 0)),
        ],
        out_specs=pl.BlockSpec((tm, n), lambda i: (i, 0)),
        out_shape=jax.ShapeDtypeStruct((t, n), F32),
        compiler_params=_compiler_params(("arbitrary",)),
        name="inproj",
    )(x2d, mod_l, norm_w, w_in_pad, gate_params)


def _modrow(mod_ref, idx):
    return mod_ref[idx]


def _causal_conv(cbuf_ref, x, w_ref, first_tile):
    ts = x.shape[0]

    @pl.when(first_tile)
    def _():
        cbuf_ref[0:SUBLANES, :] = jnp.zeros((SUBLANES, x.shape[1]), F32)

    cbuf_ref[SUBLANES:, :] = x
    base = SUBLANES - (CONV_WIDTH - 1)
    y = cbuf_ref[pl.ds(base, ts), :] * w_ref[0:1, :]
    for k in range(1, CONV_WIDTH):
        y = y + cbuf_ref[pl.ds(base + k, ts), :] * w_ref[k:k + 1, :]
    cbuf_ref[0:SUBLANES, :] = cbuf_ref[ts:ts + SUBLANES, :]
    return y


def _lru_kernel(xy_ref, cw_ref, cb_ref, gw_ref, gb_ref, lam_ref, nw_ref, o_ref,
                cbuf_ref, hprev_ref):
    t = pl.program_id(1)
    ts = xy_ref.shape[0]
    width = xy_ref.shape[1] // 2
    first = t == 0

    @pl.when(first)
    def _():
        hprev_ref[...] = jnp.zeros_like(hprev_ref)

    x_lru = xy_ref[:, :width]
    y_lru = xy_ref[:, width:]
    xr = _causal_conv(cbuf_ref, x_lru, cw_ref, first) + cb_ref[...]

    xr_b = xr.astype(BF16)
    n_blk = width // MXU_DIM
    r_parts, i_parts = [], []
    for blk in range(n_blk):
        cols = slice(blk * MXU_DIM, (blk + 1) * MXU_DIM)
        gates = jnp.dot(xr_b[:, cols], gw_ref[blk], preferred_element_type=F32)
        r_parts.append(gates[:, :MXU_DIM])
        i_parts.append(gates[:, MXU_DIM:])
    r_pre = jnp.concatenate(r_parts, axis=1) + gb_ref[0:1, :]
    i_pre = jnp.concatenate(i_parts, axis=1) + gb_ref[1:2, :]
    r = _sigmoid(r_pre)
    i = _sigmoid(i_pre)
    lam = lam_ref[...]
    log_sig_lam = jnp.minimum(lam, 0.0) - jnp.log1p(jnp.exp(-jnp.abs(lam)))
    log_a = LRU_C * r * log_sig_lam
    a = jnp.exp(log_a)
    mult = jnp.sqrt(jnp.maximum(1.0 - jnp.exp(2.0 * log_a), 1e-12))
    b = mult * (i * xr)

    row = lax.broadcasted_iota(jnp.int32, (ts, width), 0)
    b = b + jnp.where(row == 0, a * hprev_ref[...], 0.0)
    shift = 1
    while shift < ts:
        valid = row >= shift
        a_sh = pltpu.roll(a, shift, axis=0)
        b_sh = pltpu.roll(b, shift, axis=0)
        b = jnp.where(valid, a * b_sh + b, b)
        a = jnp.where(valid, a * a_sh, a)
        shift *= 2
    hprev_ref[...] = b[ts - 1:ts, :]

    y3 = y_lru * y_lru * y_lru
    gelu = 0.5 * y_lru * (1.0 + jnp.tanh(math.sqrt(2.0 / math.pi) * (y_lru + 0.044715 * y3)))
    m = b * gelu
    o_ref[...] = (m * _rms_scale(m) * nw_ref[...]).astype(o_ref.dtype)


def _lru(proj3d, conv_w, conv_b, gate_w, gate_b, lam, norm_w, *, ts):
    bsz, seq, _ = proj3d.shape
    width = conv_w.shape[1]
    n_blk = width // MXU_DIM
    return pl.pallas_call(
        _lru_kernel,
        grid=(bsz, seq // ts),
        in_specs=[
            pl.BlockSpec((None, ts, 2 * width), lambda b, t: (b, t, 0)),
            pl.BlockSpec((CONV_WIDTH, width), lambda b, t: (0, 0)),
            pl.BlockSpec((1, width), lambda b, t: (0, 0)),
            pl.BlockSpec((n_blk, MXU_DIM, 2 * MXU_DIM), lambda b, t: (0, 0, 0)),
            pl.BlockSpec((2, width), lambda b, t: (0, 0)),
            pl.BlockSpec((1, width), lambda b, t: (0, 0)),
            pl.BlockSpec((1, width), lambda b, t: (0, 0)),
        ],
        out_specs=pl.BlockSpec((None, ts, width), lambda b, t: (b, t, 0)),
        out_shape=jax.ShapeDtypeStruct((bsz, seq, width), BF16),
        scratch_shapes=[
            pltpu.VMEM((ts + SUBLANES, width), F32),
            pltpu.VMEM((1, width), F32),
        ],
        compiler_params=_compiler_params(("arbitrary", "arbitrary")),
        name="rg_lru",
    )(proj3d, conv_w, conv_b, gate_w, gate_b, lam, norm_w)


def _l2norm(t):
    return t * lax.rsqrt(jnp.sum(t * t, axis=-1, keepdims=True) + 1e-6)


def _gdn_kernel(q_ref, k_ref, v_ref, z_ref, ba_ref, cw_ref, nw_ref, o_ref,
                cbuf_q, cbuf_k, cbuf_v, qn_ref, kn_ref, vn_ref, state_ref,
                *, n_heads):
    t = pl.program_id(1)
    ts = q_ref.shape[0]
    hd = GDN_HEAD_DIM
    c = GDN_CHUNK
    width = n_heads * hd
    first = t == 0

    @pl.when(first)
    def _():
        state_ref[...] = jnp.zeros_like(state_ref)

    qc = _silu(_causal_conv(cbuf_q, q_ref[...], cw_ref.at[:, 0:width], first))
    kc = _silu(_causal_conv(cbuf_k, k_ref[...], cw_ref.at[:, width:2 * width], first))
    vn_ref[...] = _silu(_causal_conv(cbuf_v, v_ref[...], cw_ref.at[:, 2 * width:3 * width], first))
    for h in range(n_heads):
        cols = slice(h * hd, (h + 1) * hd)
        qn_ref[:, cols] = _l2norm(qc[:, cols]) * (hd ** -0.5)
        kn_ref[:, cols] = _l2norm(kc[:, cols])

    ri = lax.broadcasted_iota(jnp.int32, (c, c), 0)
    ci = lax.broadcasted_iota(jnp.int32, (c, c), 1)
    tril = (ri >= ci).astype(F32)
    causal = ri >= ci
    strict = ri > ci
    eye = (ri == ci).astype(F32)
    pair_mask = (ri == ci + 1) & ((ri & 1) == 1)
    merge_masks = []
    sz = 2
    while sz < c:
        sh = sz.bit_length() - 1
        rb = ri >> sh
        merge_masks.append((rb == (ci >> sh) + 1) & ((rb & 1) == 1))
        sz *= 2
    nw = nw_ref[...]

    def chunk_body(ic, carry):
        r0 = pl.multiple_of(ic * c, c)
        rows = pl.ds(r0, c)
        ba = ba_ref[rows, :]
        for h in range(n_heads):
            cols = slice(h * hd, (h + 1) * hd)
            q = qn_ref[rows, cols]
            k = kn_ref[rows, cols]
            v = vn_ref[rows, cols]
            beta = jnp.broadcast_to(ba[:, h:h + 1], (c, hd))
            g = jnp.broadcast_to(ba[:, n_heads + h:n_heads + h + 1], (c, hd))
            gc = jnp.dot(tril, g, precision=lax.Precision.HIGHEST,
                         preferred_element_type=F32)
            gc_row = gc.T[0:c, :]
            diff = gc[:, 0:c] - gc_row
            decay = jnp.where(causal, jnp.exp(jnp.where(causal, diff, 0.0)), 0.0)
            eg = jnp.exp(gc)
            g_last = gc[c - 1:c, :]
            kb = k * beta
            vb = v * beta
            k_b16 = k.astype(BF16)
            a_lhs = jnp.concatenate([kb, q], axis=0).astype(BF16)
            kkqk = lax.dot_general(a_lhs, k_b16, (((1,), (1,)), ((), ())),
                                   preferred_element_type=F32)
            lmat = jnp.where(strict, kkqk[0:c] * decay, 0.0)
            attn = jnp.where(causal, kkqk[c:2 * c] * decay, 0.0)
            p = eye - jnp.where(pair_mask, lmat, 0.0)
            for merge_mask in merge_masks:
                p_b = p.astype(BF16)
                cp = jnp.dot(jnp.where(merge_mask, lmat, 0.0).astype(BF16), p_b,
                             preferred_element_type=F32)
                p = p - jnp.dot(p_b, cp.astype(BF16), preferred_element_type=F32)
            rhs = jnp.concatenate([vb, kb * eg], axis=1).astype(BF16)
            uw = jnp.dot(p.astype(BF16), rhs, preferred_element_type=F32)
            u = uw[:, 0:hd]
            w = uw[:, hd:2 * hd]
            qd = q * eg
            kt = k * jnp.exp(g_last - gc)
            state = state_ref[h]
            wq = jnp.dot(jnp.concatenate([w, qd], axis=0).astype(BF16), state.astype(BF16),
                         preferred_element_type=F32)
            v_new = u - wq[0:c]
            ak = jnp.concatenate([attn, kt.T], axis=0).astype(BF16)
            od = jnp.dot(ak, v_new.astype(BF16), preferred_element_type=F32)
            o = wq[c:2 * c] + od[0:c]
            state_ref[h] = state * jnp.exp(g_last) + od[c:c + hd]
            zz = z_ref[rows, cols]
            o_ref[rows, cols] = (o * _rms_scale(o) * nw * _silu(zz)).astype(o_ref.dtype)
        return carry

    lax.fori_loop(0, ts // c, chunk_body, 0)


def _gdn(proj3d, conv_w, norm_w, *, ts, n_heads, q_col):
    bsz, seq, _ = proj3d.shape
    width = n_heads * GDN_HEAD_DIM
    qb = q_col // width
    ba_blk = (q_col + 4 * width) // LANES
    return pl.pallas_call(
        functools.partial(_gdn_kernel, n_heads=n_heads),
        grid=(bsz, seq // ts),
        in_specs=[
            pl.BlockSpec((None, ts, width), lambda b, t: (b, t, qb)),
            pl.BlockSpec((None, ts, width), lambda b, t: (b, t, qb + 1)),
            pl.BlockSpec((None, ts, width), lambda b, t: (b, t, qb + 2)),
            pl.BlockSpec((None, ts, width), lambda b, t: (b, t, qb + 3)),
            pl.BlockSpec((None, ts, LANES), lambda b, t: (b, t, ba_blk)),
            pl.BlockSpec((CONV_WIDTH, 3 * width), lambda b, t: (0, 0)),
            pl.BlockSpec((1, GDN_HEAD_DIM), lambda b, t: (0, 0)),
        ],
        out_specs=pl.BlockSpec((None, ts, width), lambda b, t: (b, t, 0)),
        out_shape=jax.ShapeDtypeStruct((bsz, seq, width), BF16),
        scratch_shapes=[
            pltpu.VMEM((ts + SUBLANES, width), F32),
            pltpu.VMEM((ts + SUBLANES, width), F32),
            pltpu.VMEM((ts + SUBLANES, width), F32),
            pltpu.VMEM((ts, width), F32),
            pltpu.VMEM((ts, width), F32),
            pltpu.VMEM((ts, width), F32),
            pltpu.VMEM((n_heads, GDN_HEAD_DIM, GDN_HEAD_DIM), F32),
        ],
        compiler_params=_compiler_params(("arbitrary", "arbitrary")),
        name="gated_deltanet",
    )(proj3d, proj3d, proj3d, proj3d, proj3d, conv_w, norm_w)


def _outmlp_kernel(x_ref, lru_ref, gdn_ref, mod_ref, wo_ref, nw_ref, wu_ref, wd_ref,
                   fnw_ref, o_ref, x1_ref, h2_ref, acc_ref, *, final_norm):
    j = pl.program_id(1)

    @pl.when(j == 0)
    def _():
        mixed = jnp.concatenate([lru_ref[...], gdn_ref[...]], axis=1)
        mix = jnp.dot(mixed, wo_ref[...], preferred_element_type=F32)
        x1 = x_ref[...] + mod_ref[2] * mix
        x1_ref[...] = x1
        h2 = (x1 * _rms_scale(x1) * nw_ref[...]) * (1.0 + mod_ref[4]) + mod_ref[3]
        h2_ref[...] = h2.astype(BF16)
        acc_ref[...] = jnp.zeros_like(acc_ref)

    up = jnp.dot(h2_ref[...], wu_ref[...], preferred_element_type=F32)
    act = jnp.square(jnp.maximum(up, 0.0)).astype(BF16)
    acc_ref[...] += jnp.dot(act, wd_ref[...], preferred_element_type=F32)

    @pl.when(j == pl.num_programs(1) - 1)
    def _():
        y = x1_ref[...] + mod_ref[5] * acc_ref[...]
        if final_norm:
            y = y * _rms_scale(y) * fnw_ref[...]
        o_ref[...] = y


def _outmlp(x2d, lru2d, gdn2d, mod_l, w_out, norm_w, w_up, w_down, final_w,
            *, seq, tm, tf, final_norm):
    t, d = x2d.shape
    dff = w_up.shape[1]
    half = lru2d.shape[1]
    tiles_per_seq = seq // tm
    return pl.pallas_call(
        functools.partial(_outmlp_kernel, final_norm=final_norm),
        grid=(t // tm, dff // tf),
        in_specs=[
            pl.BlockSpec((tm, d), lambda i, j: (i, 0)),
            pl.BlockSpec((tm, half), lambda i, j: (i, 0)),
            pl.BlockSpec((tm, half), lambda i, j: (i, 0)),
            pl.BlockSpec((None, N_MOD, 1, d), lambda i, j: (i // tiles_per_seq, 0, 0, 0)),
            pl.BlockSpec((d, d), lambda i, j: (0, 0)),
            pl.BlockSpec((1, d), lambda i, j: (0, 0)),
            pl.BlockSpec((d, tf), lambda i, j: (0, j)),
            pl.BlockSpec((tf, d), lambda i, j: (j, 0)),
            pl.BlockSpec((1, d), lambda i, j: (0, 0)),
        ],
        out_specs=pl.BlockSpec((tm, d), lambda i, j: (i, 0)),
        out_shape=jax.ShapeDtypeStruct((t, d), F32),
        scratch_shapes=[
            pltpu.VMEM((tm, d), F32),
            pltpu.VMEM((tm, d), BF16),
            pltpu.VMEM((tm, d), F32),
        ],
        compiler_params=_compiler_params(("arbitrary", "arbitrary")),
        name="outproj_mlp",
    )(x2d, lru2d, gdn2d, mod_l, w_out, norm_w, w_up, w_down, final_w)


def _block_diag_gates(gate_a, gate_x):
    g, n, _ = gate_a.shape
    per = MXU_DIM // n
    n_blk = g // per

    def diag(w):
        w = w.reshape(n_blk, per, n, n)
        eye = jnp.eye(per, dtype=w.dtype)
        return jnp.einsum("bpij,pq->bpiqj", w, eye).reshape(n_blk, MXU_DIM, MXU_DIM)

    return jnp.concatenate([diag(gate_a), diag(gate_x)], axis=2)


def kernel(x, c, norm_mix_w, norm_mlp_w, w_mod, b_mod, w_in, lru_conv_w, lru_conv_b,
           lru_gate_a_w, lru_gate_a_b, lru_gate_x_w, lru_gate_x_b, lru_lambda, lru_norm_w,
           gdn_conv_w, gdn_a_log, gdn_dt_bias, gdn_norm_w, w_out, w_up, w_down, final_norm_w):
    bsz, seq, d = x.shape
    depth = w_in.shape[0]
    lru_width = lru_conv_w.shape[2]
    n_heads = gdn_a_log.shape[1]
    gdn_width = n_heads * GDN_HEAD_DIM
    in_cols = w_in.shape[2]
    q_col = 2 * lru_width
    ba_col = q_col + 4 * gdn_width
    assert in_cols == ba_col + 2 * n_heads
    n_pad = ba_col + LANES
    tm_in, ts, tm_mlp, tf = 512, 512, 512, 1024

    mod = _modulation(c, w_mod, b_mod)
    w_in_pad = jnp.zeros((depth, d, n_pad), BF16).at[:, :, :in_cols].set(w_in.astype(BF16))
    w_out_b = w_out.astype(BF16)
    w_up_b = w_up.astype(BF16)
    w_down_b = w_down.astype(BF16)
    gate_params = jnp.zeros((depth, SUBLANES, LANES), F32)
    gate_params = gate_params.at[:, 0, n_heads:2 * n_heads].set(-jnp.exp(gdn_a_log))
    gate_params = gate_params.at[:, 1, n_heads:2 * n_heads].set(gdn_dt_bias)

    x2d = x.reshape(bsz * seq, d)
    for l in range(depth):
        proj = _inproj(x2d, mod[l], norm_mix_w[l][None], w_in_pad[l], gate_params[l],
                       seq=seq, tm=tm_in, n_heads=n_heads)
        proj3d = proj.reshape(bsz, seq, n_pad)
        gate_w = _block_diag_gates(lru_gate_a_w[l], lru_gate_x_w[l]).astype(BF16)
        gate_b = jnp.stack([lru_gate_a_b[l], lru_gate_x_b[l]])
        out_lru = _lru(proj3d, lru_conv_w[l], lru_conv_b[l][None], gate_w, gate_b,
                       lru_lambda[l][None], lru_norm_w[l][None], ts=ts)
        out_gdn = _gdn(proj3d, gdn_conv_w[l], gdn_norm_w[l][None], ts=ts, n_heads=n_heads,
                       q_col=q_col)
        x2d = _outmlp(x2d, out_lru.reshape(bsz * seq, lru_width),
                      out_gdn.reshape(bsz * seq, gdn_width), mod[l], w_out_b[l],
                      norm_mlp_w[l][None], w_up_b[l], w_down_b[l], final_norm_w[None],
                      seq=seq, tm=tm_mlp, tf=tf, final_norm=(l == depth - 1))
    return x2d.reshape(bsz, seq, d)
```

```python
import functools
import math

import jax
import jax.numpy as jnp
from jax import lax
from jax.experimental import pallas as pl
from jax.experimental.pallas import tpu as pltpu

LRU_BLOCKS = 8
LRU_C = 8.0
GDN_HEAD_DIM = 128
GDN_CHUNK = 64
CONV_WIDTH = 4
N_MOD = 6
NORM_EPS = 1e-6

LANES = 128
SUBLANES = 8
MXU_DIM = 256
VMEM_LIMIT_BYTES = 56 * 1024 * 1024

F32 = jnp.float32
BF16 = jnp.bfloat16


def _compiler_params(semantics):
    return pltpu.CompilerParams(dimension_semantics=semantics,
                                vmem_limit_bytes=VMEM_LIMIT_BYTES)


def _sigmoid(x):
    return 1.0 / (1.0 + jnp.exp(-x))


def _silu(x):
    return x * _sigmoid(x)


def _softplus(x):
    return jnp.maximum(x, 0.0) + jnp.log1p(jnp.exp(-jnp.abs(x)))


def _rms_scale(x):
    return lax.rsqrt(jnp.mean(x * x, axis=-1, keepdims=True) + NORM_EPS)


def _mod_kernel(c_ref, w_ref, b_ref, o_ref):
    c_act = _silu(c_ref[...]).astype(BF16)
    o_ref[...] = jnp.dot(c_act, w_ref[...].astype(BF16),
                         preferred_element_type=F32) + b_ref[...]


def _modulation(c, w_mod, b_mod):
    depth, d, n = w_mod.shape
    b = c.shape[0]
    bp = max(SUBLANES, b)
    c_pad = jnp.zeros((bp, d), F32).at[:b].set(c)
    tn = n // N_MOD
    out = pl.pallas_call(
        _mod_kernel,
        grid=(depth, n // tn),
        in_specs=[
            pl.BlockSpec((bp, d), lambda l, j: (0, 0)),
            pl.BlockSpec((None, d, tn), lambda l, j: (l, 0, j)),
            pl.BlockSpec((None, 1, tn), lambda l, j: (l, 0, j)),
        ],
        out_specs=pl.BlockSpec((None, bp, tn), lambda l, j: (l, 0, j)),
        out_shape=jax.ShapeDtypeStruct((depth, bp, n), F32),
        compiler_params=_compiler_params(("arbitrary", "arbitrary")),
        name="modulation",
    )(c_pad, w_mod, b_mod.reshape(depth, 1, n))
    return out[:, :b].reshape(depth, b, N_MOD, 1, d)


def _inproj_kernel(x_ref, mod_ref, nw_ref, w_ref, gp_ref, o_ref, *, n_heads):
    x = x_ref[...]
    shift = mod_ref[0]
    scale = mod_ref[1]
    h = (x * _rms_scale(x) * nw_ref[...]) * (1.0 + scale) + shift
    proj = jnp.dot(h.astype(BF16), w_ref[...], preferred_element_type=F32)
    n = proj.shape[1]
    o_ref[:, : n - LANES] = proj[:, : n - LANES]
    tail = proj[:, n - LANES:]
    lane = lax.broadcasted_iota(jnp.int32, tail.shape, 1)
    beta = _sigmoid(tail)
    neg_a = gp_ref[0:1, :]
    dt_bias = gp_ref[1:2, :]
    g = neg_a * _softplus(tail + dt_bias)
    o_ref[:, n - LANES:] = jnp.where(lane < n_heads, beta,
                                     jnp.where(lane < 2 * n_heads, g, 0.0))


def _inproj(x2d, mod_l, norm_w, w_in_pad, gate_params, *, seq, tm, n_heads):
    t, d = x2d.shape
    n = w_in_pad.shape[1]
    tiles_per_seq = seq // tm
    return pl.pallas_call(
        functools.partial(_inproj_kernel, n_heads=n_heads),
        grid=(t // tm,),
        in_specs=[
            pl.BlockSpec((tm, d), lambda i: (i, 0)),
            pl.BlockSpec((None, N_MOD, 1, d), lambda i: (i // tiles_per_seq, 0, 0, 0)),
            pl.BlockSpec((1, d), lambda i: (0, 0)),
            pl.BlockSpec((d, n), lambda i: (0, 0)),
            pl.BlockSpec((SUBLANES, LANES), lambda i: (0, 0)),
        ],
        out_specs=pl.BlockSpec((tm, n), lambda i: (i, 0)),
        out_shape=jax.ShapeDtypeStruct((t, n), F32),
        compiler_params=_compiler_params(("arbitrary",)),
        name="inproj",
    )(x2d, mod_l, norm_w, w_in_pad, gate_params)


def _modrow(mod_ref, idx):
    return mod_ref[idx]


def _causal_conv(cbuf_ref, x, w_ref, first_tile):
    ts = x.shape[0]

    @pl.when(first_tile)
    def _():
        cbuf_ref[0:SUBLANES, :] = jnp.zeros((SUBLANES, x.shape[1]), F32)

    cbuf_ref[SUBLANES:, :] = x
    base = SUBLANES - (CONV_WIDTH - 1)
    y = cbuf_ref[pl.ds(base, ts), :] * w_ref[0:1, :]
    for k in range(1, CONV_WIDTH):
        y = y + cbuf_ref[pl.ds(base + k, ts), :] * w_ref[k:k + 1, :]
    cbuf_ref[0:SUBLANES, :] = cbuf_ref[ts:ts + SUBLANES, :]
    return y


def _lru_kernel(xy_ref, cw_ref, cb_ref, gw_ref, gb_ref, lam_ref, nw_ref, o_ref,
                cbuf_ref, hprev_ref):
    t = pl.program_id(1)
    ts = xy_ref.shape[0]
    width = xy_ref.shape[1] // 2
    first = t == 0

    @pl.when(first)
    def _():
        hprev_ref[...] = jnp.zeros_like(hprev_ref)

    x_lru = xy_ref[:, :width]
    y_lru = xy_ref[:, width:]
    xr = _causal_conv(cbuf_ref, x_lru, cw_ref, first) + cb_ref[...]

    xr_b = xr.astype(BF16)
    n_blk = width // MXU_DIM
    r_parts, i_parts = [], []
    for blk in range(n_blk):
        cols = slice(blk * MXU_DIM, (blk + 1) * MXU_DIM)
        gates = jnp.dot(xr_b[:, cols], gw_ref[blk], preferred_element_type=F32)
        r_parts.append(gates[:, :MXU_DIM])
        i_parts.append(gates[:, MXU_DIM:])
    r_pre = jnp.concatenate(r_parts, axis=1) + gb_ref[0:1, :]
    i_pre = jnp.concatenate(i_parts, axis=1) + gb_ref[1:2, :]
    r = _sigmoid(r_pre)
    i = _sigmoid(i_pre)
    lam = lam_ref[...]
    log_sig_lam = jnp.minimum(lam, 0.0) - jnp.log1p(jnp.exp(-jnp.abs(lam)))
    log_a = LRU_C * r * log_sig_lam
    a = jnp.exp(log_a)
    mult = jnp.sqrt(jnp.maximum(1.0 - jnp.exp(2.0 * log_a), 1e-12))
    b = mult * (i * xr)

    row = lax.broadcasted_iota(jnp.int32, (ts, width), 0)
    b = b + jnp.where(row == 0, a * hprev_ref[...], 0.0)
    shift = 1
    while shift < ts:
        valid = row >= shift
        a_sh = pltpu.roll(a, shift, axis=0)
        b_sh = pltpu.roll(b, shift, axis=0)
        b = jnp.where(valid, a * b_sh + b, b)
        a = jnp.where(valid, a * a_sh, a)
        shift *= 2
    hprev_ref[...] = b[ts - 1:ts, :]

    y3 = y_lru * y_lru * y_lru
    gelu = 0.5 * y_lru * (1.0 + jnp.tanh(math.sqrt(2.0 / math.pi) * (y_lru + 0.044715 * y3)))
    m = b * gelu
    o_ref[...] = (m * _rms_scale(m) * nw_ref[...]).astype(o_ref.dtype)


def _lru(proj3d, conv_w, conv_b, gate_w, gate_b, lam, norm_w, *, ts):
    bsz, seq, _ = proj3d.shape
    width = conv_w.shape[1]
    n_blk = width // MXU_DIM
    return pl.pallas_call(
        _lru_kernel,
        grid=(bsz, seq // ts),
        in_specs=[
            pl.BlockSpec((None, ts, 2 * width), lambda b, t: (b, t, 0)),
            pl.BlockSpec((CONV_WIDTH, width), lambda b, t: (0, 0)),
            pl.BlockSpec((1, width), lambda b, t: (0, 0)),
            pl.BlockSpec((n_blk, MXU_DIM, 2 * MXU_DIM), lambda b, t: (0, 0, 0)),
            pl.BlockSpec((2, width), lambda b, t: (0, 0)),
            pl.BlockSpec((1, width), lambda b, t: (0, 0)),
            pl.BlockSpec((1, width), lambda b, t: (0, 0)),
        ],
        out_specs=pl.BlockSpec((None, ts, width), lambda b, t: (b, t, 0)),
        out_shape=jax.ShapeDtypeStruct((bsz, seq, width), BF16),
        scratch_shapes=[
            pltpu.VMEM((ts + SUBLANES, width), F32),
            pltpu.VMEM((1, width), F32),
        ],
        compiler_params=_compiler_params(("arbitrary", "arbitrary")),
        name="rg_lru",
    )(proj3d, conv_w, conv_b, gate_w, gate_b, lam, norm_w)


def _l2norm(t):
    return t * lax.rsqrt(jnp.sum(t * t, axis=-1, keepdims=True) + 1e-6)


def _gdn_kernel(q_ref, k_ref, v_ref, z_ref, ba_ref, cw_ref, nw_ref, o_ref,
                cbuf_q, cbuf_k, cbuf_v, kb16, q16, k16, rhs16, qd_s, kt_s, gcb_s,
                g16, h_s, qe16, ou_s, sc16, gl_s, state_ref, *, n_heads, group):
    t = pl.program_id(1)
    ts = q_ref.shape[0]
    hd = GDN_HEAD_DIM
    c = GDN_CHUNK
    n_chunks = ts // c
    width = n_heads * hd
    first = t == 0

    @pl.when(first)
    def _():
        state_ref[...] = jnp.zeros_like(state_ref)

    ri = lax.broadcasted_iota(jnp.int32, (c, c), 0)
    ci = lax.broadcasted_iota(jnp.int32, (c, c), 1)
    tril = (ri >= ci).astype(F32)
    causal = ri >= ci
    strict = ri > ci
    eye = (ri == ci).astype(F32)
    pair_mask = (ri == ci + 1) & ((ri & 1) == 1)
    merge_masks = []
    sz = 2
    while sz < c:
        sh = sz.bit_length() - 1
        rb = ri >> sh
        merge_masks.append((rb == (ci >> sh) + 1) & ((rb & 1) == 1))
        sz *= 2

    qc = _silu(_causal_conv(cbuf_q, q_ref[...], cw_ref.at[:, 0:width], first))
    kc = _silu(_causal_conv(cbuf_k, k_ref[...], cw_ref.at[:, width:2 * width], first))
    vc = _silu(_causal_conv(cbuf_v, v_ref[...], cw_ref.at[:, 2 * width:3 * width], first))
    ba = ba_ref[...]
    gc_all = jnp.concatenate(
        [jnp.dot(tril, ba[ic * c:(ic + 1) * c], precision=lax.Precision.HIGHEST,
                 preferred_element_type=F32) for ic in range(n_chunks)], axis=0)
    for h in range(n_heads):
        cols = slice(h * hd, (h + 1) * hd)
        beta = jnp.broadcast_to(ba[:, h:h + 1], (ts, hd))
        gcb = jnp.broadcast_to(gc_all[:, n_heads + h:n_heads + h + 1], (ts, hd))
        gcb3 = gcb.reshape(n_chunks, c, hd)
        g_last = gcb3[:, c - 1:c, :]
        eg = jnp.exp(gcb)
        kt_scale = jnp.exp(g_last - gcb3).reshape(ts, hd)
        qn = _l2norm(qc[:, cols]) * (hd ** -0.5)
        kn = _l2norm(kc[:, cols])
        kb = kn * beta
        kb16[:, cols] = kb.astype(BF16)
        q16[:, cols] = qn.astype(BF16)
        k16[:, cols] = kn.astype(BF16)
        rhs16[:, 2 * h * hd:(2 * h + 1) * hd] = (vc[:, cols] * beta).astype(BF16)
        rhs16[:, (2 * h + 1) * hd:(2 * h + 2) * hd] = (kb * eg).astype(BF16)
        qd_s[:, cols] = qn * eg
        kt_s[:, cols] = kn * kt_scale
        gcb_s[:, cols] = gcb
        gl_s[h] = jnp.exp(g_last).reshape(n_chunks, hd)

    def phase1_body(ig, carry):
        items = [(ig * group + j, h) for j in range(group) for h in range(n_heads)]
        rows_of = [pl.ds(pl.multiple_of(ic * c, c), c) for ic, _ in items]
        cols_of = [slice(h * hd, (h + 1) * hd) for _, h in items]
        n_items = len(items)
        lmats, attns, ps = [], [], []
        for i in range(n_items):
            rows, cols = rows_of[i], cols_of[i]
            gcb = gcb_s[rows, cols]
            diff = gcb[:, 0:c] - gcb.T[0:c, :]
            decay = jnp.where(causal, jnp.exp(jnp.where(causal, diff, 0.0)), 0.0)
            a_lhs = jnp.concatenate([kb16[rows, cols], q16[rows, cols]], axis=0)
            kkqk = lax.dot_general(a_lhs, k16[rows, cols], (((1,), (1,)), ((), ())),
                                   preferred_element_type=F32)
            lmat = jnp.where(strict, kkqk[0:c] * decay, 0.0)
            lmats.append(lmat)
            attns.append(jnp.where(causal, kkqk[c:2 * c] * decay, 0.0).astype(BF16))
            ps.append(eye - jnp.where(pair_mask, lmat, 0.0))
        for merge_mask in merge_masks:
            p_bs = [p.astype(BF16) for p in ps]
            cps = [jnp.dot(jnp.where(merge_mask, lmats[i], 0.0).astype(BF16), p_bs[i],
                           preferred_element_type=F32) for i in range(n_items)]
            ps = [ps[i] - jnp.dot(p_bs[i], cps[i].astype(BF16), preferred_element_type=F32)
                  for i in range(n_items)]
        wus = []
        for i in range(n_items):
            h = items[i][1]
            uw = jnp.dot(ps[i].astype(BF16), rhs16[rows_of[i], 2 * h * hd:(2 * h + 2) * hd],
                         preferred_element_type=F32)
            wus.append(jnp.concatenate([uw[:, hd:2 * hd], uw[:, 0:hd]], axis=1).astype(BF16))
        for i in range(n_items):
            ic, h = items[i]
            rows, cols = rows_of[i], cols_of[i]
            lhs = jnp.concatenate([kt_s[rows, cols].T.astype(BF16), attns[i]], axis=0)
            prod = jnp.dot(lhs, wus[i], preferred_element_type=F32)
            g16[ic, h] = (-prod[0:hd, 0:hd]).astype(BF16)
            h_s[ic, h] = prod[0:hd, hd:2 * hd]
            qe16[ic, h] = (qd_s[rows, cols] - prod[hd:hd + c, 0:hd]).astype(BF16)
            ou_s[ic, h] = prod[hd:hd + c, hd:2 * hd]
        return carry

    lax.fori_loop(0, n_chunks // group, phase1_body, 0)

    def phase2_body(ic, carry):
        for h in range(n_heads):
            state = state_ref[h]
            s_b = state.astype(BF16)
            sc16[ic, h] = s_b
            state_ref[h] = (state * gl_s[h, pl.ds(ic, 1), :]
                            + jnp.dot(g16[ic, h], s_b, preferred_element_type=F32)
                            + h_s[ic, h])
        return carry

    lax.fori_loop(0, n_chunks, phase2_body, 0)

    nw = nw_ref[...]

    def phase3_body(ig, carry):
        for j in range(group):
            ic = ig * group + j
            rows = pl.ds(pl.multiple_of(ic * c, c), c)
            for h in range(n_heads):
                cols = slice(h * hd, (h + 1) * hd)
                o = jnp.dot(qe16[ic, h], sc16[ic, h], preferred_element_type=F32) + ou_s[ic, h]
                o_ref[rows, cols] = (o * _rms_scale(o) * nw
                                     * _silu(z_ref[rows, cols])).astype(o_ref.dtype)
        return carry

    lax.fori_loop(0, n_chunks // group, phase3_body, 0)


def _gdn(proj3d, conv_w, norm_w, *, ts, n_heads, q_col, group):
    bsz, seq, _ = proj3d.shape
    hd = GDN_HEAD_DIM
    c = GDN_CHUNK
    width = n_heads * hd
    n_chunks = ts // c
    qb = q_col // width
    ba_blk = (q_col + 4 * width) // LANES
    return pl.pallas_call(
        functools.partial(_gdn_kernel, n_heads=n_heads, group=group),
        grid=(bsz, seq // ts),
        in_specs=[
            pl.BlockSpec((None, ts, width), lambda b, t: (b, t, qb)),
            pl.BlockSpec((None, ts, width), lambda b, t: (b, t, qb + 1)),
            pl.BlockSpec((None, ts, width), lambda b, t: (b, t, qb + 2)),
            pl.BlockSpec((None, ts, width), lambda b, t: (b, t, qb + 3)),
            pl.BlockSpec((None, ts, LANES), lambda b, t: (b, t, ba_blk)),
            pl.BlockSpec((CONV_WIDTH, 3 * width), lambda b, t: (0, 0)),
            pl.BlockSpec((1, hd), lambda b, t: (0, 0)),
        ],
        out_specs=pl.BlockSpec((None, ts, width), lambda b, t: (b, t, 0)),
        out_shape=jax.ShapeDtypeStruct((bsz, seq, width), BF16),
        scratch_shapes=[
            pltpu.VMEM((ts + SUBLANES, width), F32),
            pltpu.VMEM((ts + SUBLANES, width), F32),
            pltpu.VMEM((ts + SUBLANES, width), F32),
            pltpu.VMEM((ts, width), BF16),
            pltpu.VMEM((ts, width), BF16),
            pltpu.VMEM((ts, width), BF16),
            pltpu.VMEM((ts, 2 * width), BF16),
            pltpu.VMEM((ts, width), F32),
            pltpu.VMEM((ts, width), F32),
            pltpu.VMEM((ts, width), F32),
            pltpu.VMEM((n_chunks, n_heads, hd, hd), BF16),
            pltpu.VMEM((n_chunks, n_heads, hd, hd), F32),
            pltpu.VMEM((n_chunks, n_heads, c, hd), BF16),
            pltpu.VMEM((n_chunks, n_heads, c, hd), F32),
            pltpu.VMEM((n_chunks, n_heads, hd, hd), BF16),
            pltpu.VMEM((n_heads, n_chunks, hd), F32),
            pltpu.VMEM((n_heads, hd, hd), F32),
        ],
        compiler_params=_compiler_params(("arbitrary", "arbitrary")),
        name="gated_deltanet",
    )(proj3d, proj3d, proj3d, proj3d, proj3d, conv_w, norm_w)


def _outmlp_kernel(x_ref, lru_ref, gdn_ref, mod_ref, wo_ref, nw_ref, wu_ref, wd_ref,
                   fnw_ref, o_ref, x1_ref, h2_ref, acc_ref, *, final_norm):
    j = pl.program_id(1)

    @pl.when(j == 0)
    def _():
        mixed = jnp.concatenate([lru_ref[...], gdn_ref[...]], axis=1)
        mix = jnp.dot(mixed, wo_ref[...], preferred_element_type=F32)
        x1 = x_ref[...] + mod_ref[2] * mix
        x1_ref[...] = x1
        h2 = (x1 * _rms_scale(x1) * nw_ref[...]) * (1.0 + mod_ref[4]) + mod_ref[3]
        h2_ref[...] = h2.astype(BF16)
        acc_ref[...] = jnp.zeros_like(acc_ref)

    up = jnp.dot(h2_ref[...], wu_ref[...], preferred_element_type=F32)
    act = jnp.square(jnp.maximum(up, 0.0)).astype(BF16)
    acc_ref[...] += jnp.dot(act, wd_ref[...], preferred_element_type=F32)

    @pl.when(j == pl.num_programs(1) - 1)
    def _():
        y = x1_ref[...] + mod_ref[5] * acc_ref[...]
        if final_norm:
            y = y * _rms_scale(y) * fnw_ref[...]
        o_ref[...] = y


def _outmlp(x2d, lru2d, gdn2d, mod_l, w_out, norm_w, w_up, w_down, final_w,
            *, seq, tm, tf, final_norm):
    t, d = x2d.shape
    dff = w_up.shape[1]
    half = lru2d.shape[1]
    tiles_per_seq = seq // tm
    return pl.pallas_call(
        functools.partial(_outmlp_kernel, final_norm=final_norm),
        grid=(t // tm, dff // tf),
        in_specs=[
            pl.BlockSpec((tm, d), lambda i, j: (i, 0)),
            pl.BlockSpec((tm, half), lambda i, j: (i, 0)),
            pl.BlockSpec((tm, half), lambda i, j: (i, 0)),
            pl.BlockSpec((None, N_MOD, 1, d), lambda i, j: (i // tiles_per_seq, 0, 0, 0)),
            pl.BlockSpec((d, d), lambda i, j: (0, 0)),
            pl.BlockSpec((1, d), lambda i, j: (0, 0)),
            pl.BlockSpec((d, tf), lambda i, j: (0, j)),
            pl.BlockSpec((tf, d), lambda i, j: (j, 0)),
            pl.BlockSpec((1, d), lambda i, j: (0, 0)),
        ],
        out_specs=pl.BlockSpec((tm, d), lambda i, j: (i, 0)),
        out_shape=jax.ShapeDtypeStruct((t, d), F32),
        scratch_shapes=[
            pltpu.VMEM((tm, d), F32),
            pltpu.VMEM((tm, d), BF16),
            pltpu.VMEM((tm, d), F32),
        ],
        compiler_params=_compiler_params(("arbitrary", "arbitrary")),
        name="outproj_mlp",
    )(x2d, lru2d, gdn2d, mod_l, w_out, norm_w, w_up, w_down, final_w)


def _block_diag_gates(gate_a, gate_x):
    g, n, _ = gate_a.shape
    per = MXU_DIM // n
    n_blk = g // per

    def diag(w):
        w = w.reshape(n_blk, per, n, n)
        eye = jnp.eye(per, dtype=w.dtype)
        return jnp.einsum("bpij,pq->bpiqj", w, eye).reshape(n_blk, MXU_DIM, MXU_DIM)

    return jnp.concatenate([diag(gate_a), diag(gate_x)], axis=2)


def kernel(x, c, norm_mix_w, norm_mlp_w, w_mod, b_mod, w_in, lru_conv_w, lru_conv_b,
           lru_gate_a_w, lru_gate_a_b, lru_gate_x_w, lru_gate_x_b, lru_lambda, lru_norm_w,
           gdn_conv_w, gdn_a_log, gdn_dt_bias, gdn_norm_w, w_out, w_up, w_down, final_norm_w):
    bsz, seq, d = x.shape
    depth = w_in.shape[0]
    lru_width = lru_conv_w.shape[2]
    n_heads = gdn_a_log.shape[1]
    gdn_width = n_heads * GDN_HEAD_DIM
    in_cols = w_in.shape[2]
    q_col = 2 * lru_width
    ba_col = q_col + 4 * gdn_width
    assert in_cols == ba_col + 2 * n_heads
    n_pad = ba_col + LANES
    tm_in, ts, tm_mlp, tf = 512, 512, 512, 1024

    mod = _modulation(c, w_mod, b_mod)
    w_in_pad = jnp.zeros((depth, d, n_pad), BF16).at[:, :, :in_cols].set(w_in.astype(BF16))
    w_out_b = w_out.astype(BF16)
    w_up_b = w_up.astype(BF16)
    w_down_b = w_down.astype(BF16)
    gate_params = jnp.zeros((depth, SUBLANES, LANES), F32)
    gate_params = gate_params.at[:, 0, n_heads:2 * n_heads].set(-jnp.exp(gdn_a_log))
    gate_params = gate_params.at[:, 1, n_heads:2 * n_heads].set(gdn_dt_bias)

    x2d = x.reshape(bsz * seq, d)
    for l in range(depth):
        proj = _inproj(x2d, mod[l], norm_mix_w[l][None], w_in_pad[l], gate_params[l],
                       seq=seq, tm=tm_in, n_heads=n_heads)
        proj3d = proj.reshape(bsz, seq, n_pad)
        gate_w = _block_diag_gates(lru_gate_a_w[l], lru_gate_x_w[l]).astype(BF16)
        gate_b = jnp.stack([lru_gate_a_b[l], lru_gate_x_b[l]])
        out_lru = _lru(proj3d, lru_conv_w[l], lru_conv_b[l][None], gate_w, gate_b,
                       lru_lambda[l][None], lru_norm_w[l][None], ts=ts)
        out_gdn = _gdn(proj3d, gdn_conv_w[l], gdn_norm_w[l][None], ts=ts, n_heads=n_heads,
                       q_col=q_col, group=8)
        x2d = _outmlp(x2d, out_lru.reshape(bsz * seq, lru_width),
                      out_gdn.reshape(bsz * seq, gdn_width), mod[l], w_out_b[l],
                      norm_mlp_w[l][None], w_up_b[l], w_down_b[l], final_norm_w[None],
                      seq=seq, tm=tm_mlp, tf=tf, final_norm=(l == depth - 1))
    return x2d.reshape(bsz, seq, d)
```

```python
import functools
import math

import jax
import jax.numpy as jnp
from jax import lax
from jax.experimental import pallas as pl
from jax.experimental.pallas import tpu as pltpu

LRU_BLOCKS = 8
LRU_C = 8.0
GDN_HEAD_DIM = 128
GDN_CHUNK = 64
CONV_WIDTH = 4
N_MOD = 6
NORM_EPS = 1e-6

LANES = 128
SUBLANES = 8
MXU_DIM = 256
VMEM_LIMIT_BYTES = 56 * 1024 * 1024

F32 = jnp.float32
BF16 = jnp.bfloat16


def _compiler_params(semantics):
    return pltpu.CompilerParams(dimension_semantics=semantics,
                                vmem_limit_bytes=VMEM_LIMIT_BYTES)


def _sigmoid(x):
    return 1.0 / (1.0 + jnp.exp(-x))


def _silu(x):
    return x * _sigmoid(x)


def _softplus(x):
    return jnp.maximum(x, 0.0) + jnp.log1p(jnp.exp(-jnp.abs(x)))


def _rms_scale(x):
    return lax.rsqrt(jnp.mean(x * x, axis=-1, keepdims=True) + NORM_EPS)


def _mod_kernel(c_ref, w_ref, b_ref, o_ref):
    c_act = _silu(c_ref[...]).astype(BF16)
    o_ref[...] = jnp.dot(c_act, w_ref[...].astype(BF16),
                         preferred_element_type=F32) + b_ref[...]


def _modulation(c, w_mod, b_mod):
    depth, d, n = w_mod.shape
    b = c.shape[0]
    bp = max(SUBLANES, b)
    c_pad = jnp.zeros((bp, d), F32).at[:b].set(c)
    tn = n // N_MOD
    out = pl.pallas_call(
        _mod_kernel,
        grid=(depth, n // tn),
        in_specs=[
            pl.BlockSpec((bp, d), lambda l, j: (0, 0)),
            pl.BlockSpec((None, d, tn), lambda l, j: (l, 0, j)),
            pl.BlockSpec((None, 1, tn), lambda l, j: (l, 0, j)),
        ],
        out_specs=pl.BlockSpec((None, bp, tn), lambda l, j: (l, 0, j)),
        out_shape=jax.ShapeDtypeStruct((depth, bp, n), F32),
        compiler_params=_compiler_params(("arbitrary", "arbitrary")),
        name="modulation",
    )(c_pad, w_mod, b_mod.reshape(depth, 1, n))
    return out[:, :b].reshape(depth, b, N_MOD, 1, d)


def _inproj_kernel(x_ref, mod_ref, nw_ref, w_ref, gp_ref, o_ref, *, n_heads):
    x = x_ref[...]
    shift = mod_ref[0]
    scale = mod_ref[1]
    h = (x * _rms_scale(x) * nw_ref[...]) * (1.0 + scale) + shift
    proj = jnp.dot(h.astype(BF16), w_ref[...], preferred_element_type=F32)
    n = proj.shape[1]
    o_ref[:, : n - LANES] = proj[:, : n - LANES]
    tail = proj[:, n - LANES:]
    lane = lax.broadcasted_iota(jnp.int32, tail.shape, 1)
    beta = _sigmoid(tail)
    neg_a = gp_ref[0:1, :]
    dt_bias = gp_ref[1:2, :]
    g = neg_a * _softplus(tail + dt_bias)
    o_ref[:, n - LANES:] = jnp.where(lane < n_heads, beta,
                                     jnp.where(lane < 2 * n_heads, g, 0.0))


def _inproj(x2d, mod_l, norm_w, w_in_pad, gate_params, *, seq, tm, n_heads):
    t, d = x2d.shape
    n = w_in_pad.shape[1]
    tiles_per_seq = seq // tm
    return pl.pallas_call(
        functools.partial(_inproj_kernel, n_heads=n_heads),
        grid=(t // tm,),
        in_specs=[
            pl.BlockSpec((tm, d), lambda i: (i, 0)),
            pl.BlockSpec((None, N_MOD, 1, d), lambda i: (i // tiles_per_seq, 0, 0, 0)),
            pl.BlockSpec((1, d), lambda i: (0, 0)),
            pl.BlockSpec((d, n), lambda i: (0, 0)),
            pl.BlockSpec((SUBLANES, LANES), lambda i: (0, 0)),
        ],
        out_specs=pl.BlockSpec((tm, n), lambda i: (i, 0)),
        out_shape=jax.ShapeDtypeStruct((t, n), F32),
        compiler_params=_compiler_params(("arbitrary",)),
        name="inproj",
    )(x2d, mod_l, norm_w, w_in_pad, gate_params)


def _modrow(mod_ref, idx):
    return mod_ref[idx]


def _causal_conv(cbuf_ref, x, w_ref, first_tile):
    ts = x.shape[0]

    @pl.when(first_tile)
    def _():
        cbuf_ref[0:SUBLANES, :] = jnp.zeros((SUBLANES, x.shape[1]), F32)

    cbuf_ref[SUBLANES:, :] = x
    base = SUBLANES - (CONV_WIDTH - 1)
    y = cbuf_ref[pl.ds(base, ts), :] * w_ref[0:1, :]
    for k in range(1, CONV_WIDTH):
        y = y + cbuf_ref[pl.ds(base + k, ts), :] * w_ref[k:k + 1, :]
    cbuf_ref[0:SUBLANES, :] = cbuf_ref[ts:ts + SUBLANES, :]
    return y


def _lru_kernel(xy_ref, cw_ref, cb_ref, gw_ref, gb_ref, lam_ref, nw_ref, o_ref,
                cbuf_ref, hprev_ref):
    t = pl.program_id(1)
    ts = xy_ref.shape[0]
    width = xy_ref.shape[1] // 2
    first = t == 0

    @pl.when(first)
    def _():
        hprev_ref[...] = jnp.zeros_like(hprev_ref)

    x_lru = xy_ref[:, :width]
    y_lru = xy_ref[:, width:]
    xr = _causal_conv(cbuf_ref, x_lru, cw_ref, first) + cb_ref[...]

    xr_b = xr.astype(BF16)
    n_blk = width // MXU_DIM
    r_parts, i_parts = [], []
    for blk in range(n_blk):
        cols = slice(blk * MXU_DIM, (blk + 1) * MXU_DIM)
        gates = jnp.dot(xr_b[:, cols], gw_ref[blk], preferred_element_type=F32)
        r_parts.append(gates[:, :MXU_DIM])
        i_parts.append(gates[:, MXU_DIM:])
    r_pre = jnp.concatenate(r_parts, axis=1) + gb_ref[0:1, :]
    i_pre = jnp.concatenate(i_parts, axis=1) + gb_ref[1:2, :]
    r = _sigmoid(r_pre)
    i = _sigmoid(i_pre)
    lam = lam_ref[...]
    log_sig_lam = jnp.minimum(lam, 0.0) - jnp.log1p(jnp.exp(-jnp.abs(lam)))
    log_a = LRU_C * r * log_sig_lam
    a = jnp.exp(log_a)
    mult = jnp.sqrt(jnp.maximum(1.0 - jnp.exp(2.0 * log_a), 1e-12))
    b = mult * (i * xr)

    row = lax.broadcasted_iota(jnp.int32, (ts, width), 0)
    b = b + jnp.where(row == 0, a * hprev_ref[...], 0.0)
    shift = 1
    while shift < ts:
        valid = row >= shift
        a_sh = pltpu.roll(a, shift, axis=0)
        b_sh = pltpu.roll(b, shift, axis=0)
        b = jnp.where(valid, a * b_sh + b, b)
        a = jnp.where(valid, a * a_sh, a)
        shift *= 2
    hprev_ref[...] = b[ts - 1:ts, :]

    y3 = y_lru * y_lru * y_lru
    gelu = 0.5 * y_lru * (1.0 + jnp.tanh(math.sqrt(2.0 / math.pi) * (y_lru + 0.044715 * y3)))
    m = b * gelu
    o_ref[...] = (m * _rms_scale(m) * nw_ref[...]).astype(o_ref.dtype)


def _lru(proj3d, conv_w, conv_b, gate_w, gate_b, lam, norm_w, *, ts):
    bsz, seq, _ = proj3d.shape
    width = conv_w.shape[1]
    n_blk = width // MXU_DIM
    return pl.pallas_call(
        _lru_kernel,
        grid=(bsz, seq // ts),
        in_specs=[
            pl.BlockSpec((None, ts, 2 * width), lambda b, t: (b, t, 0)),
            pl.BlockSpec((CONV_WIDTH, width), lambda b, t: (0, 0)),
            pl.BlockSpec((1, width), lambda b, t: (0, 0)),
            pl.BlockSpec((n_blk, MXU_DIM, 2 * MXU_DIM), lambda b, t: (0, 0, 0)),
            pl.BlockSpec((2, width), lambda b, t: (0, 0)),
            pl.BlockSpec((1, width), lambda b, t: (0, 0)),
            pl.BlockSpec((1, width), lambda b, t: (0, 0)),
        ],
        out_specs=pl.BlockSpec((None, ts, width), lambda b, t: (b, t, 0)),
        out_shape=jax.ShapeDtypeStruct((bsz, seq, width), BF16),
        scratch_shapes=[
            pltpu.VMEM((ts + SUBLANES, width), F32),
            pltpu.VMEM((1, width), F32),
        ],
        compiler_params=_compiler_params(("arbitrary", "arbitrary")),
        name="rg_lru",
    )(proj3d, conv_w, conv_b, gate_w, gate_b, lam, norm_w)


def _l2norm(t):
    return t * lax.rsqrt(jnp.sum(t * t, axis=-1, keepdims=True) + 1e-6)


def _gdn_kernel(q_ref, k_ref, v_ref, z_ref, ba_ref, cw_ref, nw_ref, o_ref,
                cbuf_q, cbuf_k, cbuf_v, kb16, q16, k16, rhs16, qd_s, kt_s, gcb_s,
                g16, h_s, qe16, ou_s, sc16, gl_s, state_ref, *, n_heads, group):
    t = pl.program_id(1)
    ts = q_ref.shape[0]
    hd = GDN_HEAD_DIM
    c = GDN_CHUNK
    n_chunks = ts // c
    width = n_heads * hd
    first = t == 0

    @pl.when(first)
    def _():
        state_ref[...] = jnp.zeros_like(state_ref)

    ri = lax.broadcasted_iota(jnp.int32, (c, c), 0)
    ci = lax.broadcasted_iota(jnp.int32, (c, c), 1)
    tril = (ri >= ci).astype(F32)
    causal = ri >= ci
    strict = ri > ci
    eye = (ri == ci).astype(F32)
    pair_mask = (ri == ci + 1) & ((ri & 1) == 1)
    merge_masks = []
    sz = 2
    while sz < c:
        sh = sz.bit_length() - 1
        rb = ri >> sh
        merge_masks.append((rb == (ci >> sh) + 1) & ((rb & 1) == 1))
        sz *= 2

    qc = _silu(_causal_conv(cbuf_q, q_ref[...], cw_ref.at[:, 0:width], first))
    kc = _silu(_causal_conv(cbuf_k, k_ref[...], cw_ref.at[:, width:2 * width], first))
    vc = _silu(_causal_conv(cbuf_v, v_ref[...], cw_ref.at[:, 2 * width:3 * width], first))
    ba = ba_ref[...]
    gc_all = jnp.concatenate(
        [jnp.dot(tril, ba[ic * c:(ic + 1) * c], precision=lax.Precision.HIGHEST,
                 preferred_element_type=F32) for ic in range(n_chunks)], axis=0)
    for h in range(n_heads):
        cols = slice(h * hd, (h + 1) * hd)
        beta = jnp.broadcast_to(ba[:, h:h + 1], (ts, hd))
        gcb = jnp.broadcast_to(gc_all[:, n_heads + h:n_heads + h + 1], (ts, hd))
        gcb3 = gcb.reshape(n_chunks, c, hd)
        g_last = gcb3[:, c - 1:c, :]
        eg = jnp.exp(gcb)
        kt_scale = jnp.exp(g_last - gcb3).reshape(ts, hd)
        qn = _l2norm(qc[:, cols]) * (hd ** -0.5)
        kn = _l2norm(kc[:, cols])
        kb = kn * beta
        kb16[:, cols] = kb.astype(BF16)
        q16[:, cols] = qn.astype(BF16)
        k16[:, cols] = kn.astype(BF16)
        rhs16[:, 2 * h * hd:(2 * h + 1) * hd] = (vc[:, cols] * beta).astype(BF16)
        rhs16[:, (2 * h + 1) * hd:(2 * h + 2) * hd] = (kb * eg).astype(BF16)
        qd_s[:, cols] = qn * eg
        kt_s[:, cols] = kn * kt_scale
        gcb_s[:, cols] = gcb
        gl_s[h] = jnp.exp(g_last).reshape(n_chunks, hd)

    def phase1_body(ig, carry):
        items = [(ig * group + j, h) for j in range(group) for h in range(n_heads)]
        rows_of = [pl.ds(pl.multiple_of(ic * c, c), c) for ic, _ in items]
        cols_of = [slice(h * hd, (h + 1) * hd) for _, h in items]
        n_items = len(items)
        lmats, attns, ps = [], [], []
        for i in range(n_items):
            rows, cols = rows_of[i], cols_of[i]
            gcb = gcb_s[rows, cols]
            diff = gcb[:, 0:c] - gcb.T[0:c, :]
            decay = jnp.where(causal, jnp.exp(jnp.where(causal, diff, 0.0)), 0.0)
            a_lhs = jnp.concatenate([kb16[rows, cols], q16[rows, cols]], axis=0)
            kkqk = lax.dot_general(a_lhs, k16[rows, cols], (((1,), (1,)), ((), ())),
                                   preferred_element_type=F32)
            lmat = jnp.where(strict, kkqk[0:c] * decay, 0.0)
            lmats.append(lmat)
            attns.append(jnp.where(causal, kkqk[c:2 * c] * decay, 0.0).astype(BF16))
            ps.append(eye - jnp.where(pair_mask, lmat, 0.0))
        for merge_mask in merge_masks:
            p_bs = [p.astype(BF16) for p in ps]
            cps = [jnp.dot(jnp.where(merge_mask, lmats[i], 0.0).astype(BF16), p_bs[i],
                           preferred_element_type=F32) for i in range(n_items)]
            ps = [ps[i] - jnp.dot(p_bs[i], cps[i].astype(BF16), preferred_element_type=F32)
                  for i in range(n_items)]
        wus = []
        for i in range(n_items):
            h = items[i][1]
            uw = jnp.dot(ps[i].astype(BF16), rhs16[rows_of[i], 2 * h * hd:(2 * h + 2) * hd],
                         preferred_element_type=F32)
            wus.append(jnp.concatenate([uw[:, hd:2 * hd], uw[:, 0:hd]], axis=1).astype(BF16))
        for i in range(n_items):
            ic, h = items[i]
            rows, cols = rows_of[i], cols_of[i]
            lhs = jnp.concatenate([kt_s[rows, cols].T.astype(BF16), attns[i]], axis=0)
            prod = jnp.dot(lhs, wus[i], preferred_element_type=F32)
            g16[ic, h] = (-prod[0:hd, 0:hd]).astype(BF16)
            h_s[ic, h] = prod[0:hd, hd:2 * hd]
            qe16[ic, h] = (qd_s[rows, cols] - prod[hd:hd + c, 0:hd]).astype(BF16)
            ou_s[ic, h] = prod[hd:hd + c, hd:2 * hd]
        return carry

    lax.fori_loop(0, n_chunks // group, phase1_body, 0)

    def phase2_body(ic, carry):
        for h in range(n_heads):
            state = state_ref[h]
            s_b = state.astype(BF16)
            sc16[ic, h] = s_b
            state_ref[h] = (state * gl_s[h, pl.ds(ic, 1), :]
                            + jnp.dot(g16[ic, h], s_b, preferred_element_type=F32)
                            + h_s[ic, h])
        return carry

    lax.fori_loop(0, n_chunks, phase2_body, 0)

    nw = nw_ref[...]

    def phase3_body(ig, carry):
        for j in range(group):
            ic = ig * group + j
            rows = pl.ds(pl.multiple_of(ic * c, c), c)
            for h in range(n_heads):
                cols = slice(h * hd, (h + 1) * hd)
                o = jnp.dot(qe16[ic, h], sc16[ic, h], preferred_element_type=F32) + ou_s[ic, h]
                o_ref[rows, cols] = (o * _rms_scale(o) * nw
                                     * _silu(z_ref[rows, cols])).astype(o_ref.dtype)
        return carry

    lax.fori_loop(0, n_chunks // group, phase3_body, 0)


def _gdn(proj3d, conv_w, norm_w, *, ts, n_heads, q_col, group):
    bsz, seq, _ = proj3d.shape
    hd = GDN_HEAD_DIM
    c = GDN_CHUNK
    width = n_heads * hd
    n_chunks = ts // c
    qb = q_col // width
    ba_blk = (q_col + 4 * width) // LANES
    return pl.pallas_call(
        functools.partial(_gdn_kernel, n_heads=n_heads, group=group),
        grid=(bsz, seq // ts),
        in_specs=[
            pl.BlockSpec((None, ts, width), lambda b, t: (b, t, qb)),
            pl.BlockSpec((None, ts, width), lambda b, t: (b, t, qb + 1)),
            pl.BlockSpec((None, ts, width), lambda b, t: (b, t, qb + 2)),
            pl.BlockSpec((None, ts, width), lambda b, t: (b, t, qb + 3)),
            pl.BlockSpec((None, ts, LANES), lambda b, t: (b, t, ba_blk)),
            pl.BlockSpec((CONV_WIDTH, 3 * width), lambda b, t: (0, 0)),
            pl.BlockSpec((1, hd), lambda b, t: (0, 0)),
        ],
        out_specs=pl.BlockSpec((None, ts, width), lambda b, t: (b, t, 0)),
        out_shape=jax.ShapeDtypeStruct((bsz, seq, width), BF16),
        scratch_shapes=[
            pltpu.VMEM((ts + SUBLANES, width), F32),
            pltpu.VMEM((ts + SUBLANES, width), F32),
            pltpu.VMEM((ts + SUBLANES, width), F32),
            pltpu.VMEM((ts, width), BF16),
            pltpu.VMEM((ts, width), BF16),
            pltpu.VMEM((ts, width), BF16),
            pltpu.VMEM((ts, 2 * width), BF16),
            pltpu.VMEM((ts, width), F32),
            pltpu.VMEM((ts, width), F32),
            pltpu.VMEM((ts, width), F32),
            pltpu.VMEM((n_chunks, n_heads, hd, hd), BF16),
            pltpu.VMEM((n_chunks, n_heads, hd, hd), F32),
            pltpu.VMEM((n_chunks, n_heads, c, hd), BF16),
            pltpu.VMEM((n_chunks, n_heads, c, hd), F32),
            pltpu.VMEM((n_chunks, n_heads, hd, hd), BF16),
            pltpu.VMEM((n_heads, n_chunks, hd), F32),
            pltpu.VMEM((n_heads, hd, hd), F32),
        ],
        compiler_params=_compiler_params(("arbitrary", "arbitrary")),
        name="gated_deltanet",
    )(proj3d, proj3d, proj3d, proj3d, proj3d, conv_w, norm_w)


def _outmlp_kernel(x_ref, lru_ref, gdn_ref, mod_ref, wo_ref, nw_ref, wu_ref, wd_ref,
                   fnw_ref, o_ref, *, final_norm, tf):
    mixed = jnp.concatenate([lru_ref[...], gdn_ref[...]], axis=1)
    mix = jnp.dot(mixed, wo_ref[...], preferred_element_type=F32)
    x1 = x_ref[...] + mod_ref[2] * mix
    h2 = ((x1 * _rms_scale(x1) * nw_ref[...]) * (1.0 + mod_ref[4]) + mod_ref[3]).astype(BF16)
    dff = wu_ref.shape[1]
    acc = None
    for j in range(dff // tf):
        up = jnp.dot(h2, wu_ref[:, j * tf:(j + 1) * tf], preferred_element_type=F32)
        act = jnp.square(jnp.maximum(up, 0.0)).astype(BF16)
        down = jnp.dot(act, wd_ref[j * tf:(j + 1) * tf, :], preferred_element_type=F32)
        acc = down if acc is None else acc + down
    y = x1 + mod_ref[5] * acc
    if final_norm:
        y = y * _rms_scale(y) * fnw_ref[...]
    o_ref[...] = y


def _resident(shape):
    return pl.BlockSpec(shape, lambda i: (0,) * len(shape), pipeline_mode=pl.Buffered(1))


def _outmlp(x2d, lru2d, gdn2d, mod_l, w_out, norm_w, w_up, w_down, final_w,
            *, seq, tm, tf, final_norm):
    t, d = x2d.shape
    dff = w_up.shape[1]
    half = lru2d.shape[1]
    tiles_per_seq = seq // tm
    return pl.pallas_call(
        functools.partial(_outmlp_kernel, final_norm=final_norm, tf=tf),
        grid=(t // tm,),
        in_specs=[
            pl.BlockSpec((tm, d), lambda i: (i, 0)),
            pl.BlockSpec((tm, half), lambda i: (i, 0)),
            pl.BlockSpec((tm, half), lambda i: (i, 0)),
            pl.BlockSpec((None, N_MOD, 1, d), lambda i: (i // tiles_per_seq, 0, 0, 0)),
            _resident((d, d)),
            _resident((1, d)),
            _resident((d, dff)),
            _resident((dff, d)),
            _resident((1, d)),
        ],
        out_specs=pl.BlockSpec((tm, d), lambda i: (i, 0)),
        out_shape=jax.ShapeDtypeStruct((t, d), F32),
        compiler_params=_compiler_params(("arbitrary",)),
        name="outproj_mlp",
    )(x2d, lru2d, gdn2d, mod_l, w_out, norm_w, w_up, w_down, final_w)


def _block_diag_gates(gate_a, gate_x):
    g, n, _ = gate_a.shape
    per = MXU_DIM // n
    n_blk = g // per

    def diag(w):
        w = w.reshape(n_blk, per, n, n)
        eye = jnp.eye(per, dtype=w.dtype)
        return jnp.einsum("bpij,pq->bpiqj", w, eye).reshape(n_blk, MXU_DIM, MXU_DIM)

    return jnp.concatenate([diag(gate_a), diag(gate_x)], axis=2)


def kernel(x, c, norm_mix_w, norm_mlp_w, w_mod, b_mod, w_in, lru_conv_w, lru_conv_b,
           lru_gate_a_w, lru_gate_a_b, lru_gate_x_w, lru_gate_x_b, lru_lambda, lru_norm_w,
           gdn_conv_w, gdn_a_log, gdn_dt_bias, gdn_norm_w, w_out, w_up, w_down, final_norm_w):
    bsz, seq, d = x.shape
    depth = w_in.shape[0]
    lru_width = lru_conv_w.shape[2]
    n_heads = gdn_a_log.shape[1]
    gdn_width = n_heads * GDN_HEAD_DIM
    in_cols = w_in.shape[2]
    q_col = 2 * lru_width
    ba_col = q_col + 4 * gdn_width
    assert in_cols == ba_col + 2 * n_heads
    n_pad = ba_col + LANES
    tm_in, ts, tm_mlp, tf = 512, 512, 512, 1024

    mod = _modulation(c, w_mod, b_mod)
    w_in_pad = jnp.zeros((depth, d, n_pad), BF16).at[:, :, :in_cols].set(w_in.astype(BF16))
    w_out_b = w_out.astype(BF16)
    w_up_b = w_up.astype(BF16)
    w_down_b = w_down.astype(BF16)
    gate_params = jnp.zeros((depth, SUBLANES, LANES), F32)
    gate_params = gate_params.at[:, 0, n_heads:2 * n_heads].set(-jnp.exp(gdn_a_log))
    gate_params = gate_params.at[:, 1, n_heads:2 * n_heads].set(gdn_dt_bias)

    x2d = x.reshape(bsz * seq, d)
    for l in range(depth):
        proj = _inproj(x2d, mod[l], norm_mix_w[l][None], w_in_pad[l], gate_params[l],
                       seq=seq, tm=tm_in, n_heads=n_heads)
        proj3d = proj.reshape(bsz, seq, n_pad)
        gate_w = _block_diag_gates(lru_gate_a_w[l], lru_gate_x_w[l]).astype(BF16)
        gate_b = jnp.stack([lru_gate_a_b[l], lru_gate_x_b[l]])
        out_lru = _lru(proj3d, lru_conv_w[l], lru_conv_b[l][None], gate_w, gate_b,
                       lru_lambda[l][None], lru_norm_w[l][None], ts=ts)
        out_gdn = _gdn(proj3d, gdn_conv_w[l], gdn_norm_w[l][None], ts=ts, n_heads=n_heads,
                       q_col=q_col, group=8)
        x2d = _outmlp(x2d, out_lru.reshape(bsz * seq, lru_width),
                      out_gdn.reshape(bsz * seq, gdn_width), mod[l], w_out_b[l],
                      norm_mlp_w[l][None], w_up_b[l], w_down_b[l], final_norm_w[None],
                      seq=seq, tm=tm_mlp, tf=tf, final_norm=(l == depth - 1))
    return x2d.reshape(bsz, seq, d)
```

```python
import functools
import math

import jax
import jax.numpy as jnp
from jax import lax
from jax.experimental import pallas as pl
from jax.experimental.pallas import tpu as pltpu

LRU_C = 8.0
GDN_HEAD_DIM = 128
GDN_CHUNK = 64
CONV_WIDTH = 4
N_MOD = 6
NORM_EPS = 1e-6

LANES = 128
SUBLANES = 8
MXU_DIM = 256
VMEM_LIMIT_BYTES = 56 * 1024 * 1024

MIXER_ROWS = 512
MLP_ROWS = 512
MLP_FF_CHUNK = 1024

F32 = jnp.float32
BF16 = jnp.bfloat16


def _compiler_params(semantics):
    return pltpu.CompilerParams(dimension_semantics=semantics,
                                vmem_limit_bytes=VMEM_LIMIT_BYTES)


def _resident(shape, n_grid_axes):
    zeros = (0,) * len(shape)
    if n_grid_axes == 1:
        index_map = lambda i: zeros
    else:
        index_map = lambda i, j: zeros
    return pl.BlockSpec(shape, index_map, pipeline_mode=pl.Buffered(1))


def _sigmoid(x):
    return 1.0 / (1.0 + jnp.exp(-x))


def _silu(x):
    return x * _sigmoid(x)


def _softplus(x):
    return jnp.maximum(x, 0.0) + jnp.log1p(jnp.exp(-jnp.abs(x)))


def _rms_scale(x):
    return lax.rsqrt(jnp.mean(x * x, axis=-1, keepdims=True) + NORM_EPS)


def _l2norm(t):
    return t * lax.rsqrt(jnp.sum(t * t, axis=-1, keepdims=True) + 1e-6)


def _mod_kernel(c_ref, w_ref, b_ref, o_ref):
    c_act = _silu(c_ref[...]).astype(BF16)
    o_ref[...] = jnp.dot(c_act, w_ref[...].astype(BF16),
                         preferred_element_type=F32) + b_ref[...]


def _modulation(c, w_mod, b_mod):
    depth, d, n = w_mod.shape
    b = c.shape[0]
    bp = max(SUBLANES, b)
    c_pad = jnp.zeros((bp, d), F32).at[:b].set(c)
    tn = n // N_MOD
    out = pl.pallas_call(
        _mod_kernel,
        grid=(depth, n // tn),
        in_specs=[
            pl.BlockSpec((bp, d), lambda l, j: (0, 0)),
            pl.BlockSpec((None, d, tn), lambda l, j: (l, 0, j)),
            pl.BlockSpec((None, 1, tn), lambda l, j: (l, 0, j)),
        ],
        out_specs=pl.BlockSpec((None, bp, tn), lambda l, j: (l, 0, j)),
        out_shape=jax.ShapeDtypeStruct((depth, bp, n), F32),
        compiler_params=_compiler_params(("arbitrary", "arbitrary")),
        name="modulation",
    )(c_pad, w_mod, b_mod.reshape(depth, 1, n))
    return out[:, :b].reshape(depth, b, N_MOD, 1, d)


def _causal_conv(cbuf_ref, x, w_ref):
    ts = x.shape[0]
    cbuf_ref[SUBLANES:, :] = x
    base = SUBLANES - (CONV_WIDTH - 1)
    y = cbuf_ref[pl.ds(base, ts), :] * w_ref[0:1, :]
    for k in range(1, CONV_WIDTH):
        y = y + cbuf_ref[pl.ds(base + k, ts), :] * w_ref[k:k + 1, :]
    cbuf_ref[0:SUBLANES, :] = cbuf_ref[ts:ts + SUBLANES, :]
    return y


def _lru_branch(x_lru, y_lru, cw_ref, cb_ref, gw_ref, gb_ref, lam_ref, nw_ref,
                cbuf_ref, hprev_ref):
    ts, width = x_lru.shape
    xr = _causal_conv(cbuf_ref, x_lru, cw_ref) + cb_ref[...]

    xr_b = xr.astype(BF16)
    r_parts, i_parts = [], []
    for blk in range(width // MXU_DIM):
        cols = slice(blk * MXU_DIM, (blk + 1) * MXU_DIM)
        gates = jnp.dot(xr_b[:, cols], gw_ref[blk], preferred_element_type=F32)
        r_parts.append(gates[:, :MXU_DIM])
        i_parts.append(gates[:, MXU_DIM:])
    r = _sigmoid(jnp.concatenate(r_parts, axis=1) + gb_ref[0:1, :])
    i = _sigmoid(jnp.concatenate(i_parts, axis=1) + gb_ref[1:2, :])
    lam = lam_ref[...]
    log_sig_lam = jnp.minimum(lam, 0.0) - jnp.log1p(jnp.exp(-jnp.abs(lam)))
    log_a = LRU_C * r * log_sig_lam
    a = jnp.exp(log_a)
    mult = jnp.sqrt(jnp.maximum(1.0 - jnp.exp(2.0 * log_a), 1e-12))
    b = mult * (i * xr)

    row = lax.broadcasted_iota(jnp.int32, (ts, width), 0)
    b = b + jnp.where(row == 0, a * hprev_ref[...], 0.0)
    shift = 1
    while shift < ts:
        valid = row >= shift
        a_sh = pltpu.roll(a, shift, axis=0)
        b_sh = pltpu.roll(b, shift, axis=0)
        b = jnp.where(valid, a * b_sh + b, b)
        a = jnp.where(valid, a * a_sh, a)
        shift *= 2
    hprev_ref[...] = b[ts - 1:ts, :]

    y3 = y_lru * y_lru * y_lru
    gelu = 0.5 * y_lru * (1.0 + jnp.tanh(math.sqrt(2.0 / math.pi) * (y_lru + 0.044715 * y3)))
    m = b * gelu
    return m * _rms_scale(m) * nw_ref[...]


def _gdn_branch(q_pre, k_pre, v_pre, ba, cw_ref, nw_ref, sz_ref, out_ref, out_col,
                cbuf_q, cbuf_k, cbuf_v, kb16, q16, k16, rhs16, qd_s, kt_s, gcb_s,
                g16, h_s, qe16, ou_s, sc16, gl_s, state_ref, *, n_heads):
    ts = q_pre.shape[0]
    hd = GDN_HEAD_DIM
    c = GDN_CHUNK
    n_chunks = ts // c
    width = n_heads * hd

    ri = lax.broadcasted_iota(jnp.int32, (c, c), 0)
    ci = lax.broadcasted_iota(jnp.int32, (c, c), 1)
    tril = (ri >= ci).astype(F32)
    causal = ri >= ci
    strict = ri > ci
    eye = (ri == ci).astype(F32)
    pair_mask = (ri == ci + 1) & ((ri & 1) == 1)
    merge_masks = []
    sz = 2
    while sz < c:
        sh = sz.bit_length() - 1
        rb = ri >> sh
        merge_masks.append((rb == (ci >> sh) + 1) & ((rb & 1) == 1))
        sz *= 2

    qc = _silu(_causal_conv(cbuf_q, q_pre, cw_ref.at[:, 0:width]))
    kc = _silu(_causal_conv(cbuf_k, k_pre, cw_ref.at[:, width:2 * width]))
    vc = _silu(_causal_conv(cbuf_v, v_pre, cw_ref.at[:, 2 * width:3 * width]))
    gc_all = jnp.concatenate(
        [jnp.dot(tril, ba[ic * c:(ic + 1) * c], precision=lax.Precision.HIGHEST,
                 preferred_element_type=F32) for ic in range(n_chunks)], axis=0)
    for h in range(n_heads):
        cols = slice(h * hd, (h + 1) * hd)
        beta = jnp.broadcast_to(ba[:, h:h + 1], (ts, hd))
        gcb = jnp.broadcast_to(gc_all[:, n_heads + h:n_heads + h + 1], (ts, hd))
        gcb3 = gcb.reshape(n_chunks, c, hd)
        g_last = gcb3[:, c - 1:c, :]
        eg = jnp.exp(gcb)
        kt_scale = jnp.exp(g_last - gcb3).reshape(ts, hd)
        qn = _l2norm(qc[:, cols]) * (hd ** -0.5)
        kn = _l2norm(kc[:, cols])
        kb = kn * beta
        kb16[:, cols] = kb.astype(BF16)
        q16[:, cols] = qn.astype(BF16)
        k16[:, cols] = kn.astype(BF16)
        rhs16[:, 2 * h * hd:(2 * h + 1) * hd] = (vc[:, cols] * beta).astype(BF16)
        rhs16[:, (2 * h + 1) * hd:(2 * h + 2) * hd] = (kb * eg).astype(BF16)
        qd_s[:, cols] = qn * eg
        kt_s[:, cols] = kn * kt_scale
        gcb_s[:, cols] = gcb
        gl_s[h] = jnp.exp(g_last).reshape(n_chunks, hd)

    items = [(ic, h) for ic in range(n_chunks) for h in range(n_heads)]
    rows_of = [slice(ic * c, (ic + 1) * c) for ic, _ in items]
    cols_of = [slice(h * hd, (h + 1) * hd) for _, h in items]
    n_items = len(items)
    lmats, attns, ps = [], [], []
    for i in range(n_items):
        rows, cols = rows_of[i], cols_of[i]
        gcb = gcb_s[rows, cols]
        diff = gcb[:, 0:c] - gcb.T[0:c, :]
        decay = jnp.where(causal, jnp.exp(jnp.where(causal, diff, 0.0)), 0.0)
        a_lhs = jnp.concatenate([kb16[rows, cols], q16[rows, cols]], axis=0)
        kkqk = lax.dot_general(a_lhs, k16[rows, cols], (((1,), (1,)), ((), ())),
                               preferred_element_type=F32)
        lmat = jnp.where(strict, kkqk[0:c] * decay, 0.0)
        lmats.append(lmat)
        attns.append(jnp.where(causal, kkqk[c:2 * c] * decay, 0.0).astype(BF16))
        ps.append(eye - jnp.where(pair_mask, lmat, 0.0))
    for merge_mask in merge_masks:
        p_bs = [p.astype(BF16) for p in ps]
        cps = [jnp.dot(jnp.where(merge_mask, lmats[i], 0.0).astype(BF16), p_bs[i],
                       preferred_element_type=F32) for i in range(n_items)]
        ps = [ps[i] - jnp.dot(p_bs[i], cps[i].astype(BF16), preferred_element_type=F32)
              for i in range(n_items)]
    wus = []
    for i in range(n_items):
        h = items[i][1]
        uw = jnp.dot(ps[i].astype(BF16), rhs16[rows_of[i], 2 * h * hd:(2 * h + 2) * hd],
                     preferred_element_type=F32)
        wus.append(jnp.concatenate([uw[:, hd:2 * hd], uw[:, 0:hd]], axis=1).astype(BF16))
    for i in range(n_items):
        ic, h = items[i]
        rows, cols = rows_of[i], cols_of[i]
        lhs = jnp.concatenate([kt_s[rows, cols].T.astype(BF16), attns[i]], axis=0)
        prod = jnp.dot(lhs, wus[i], preferred_element_type=F32)
        g16[ic, h] = (-prod[0:hd, 0:hd]).astype(BF16)
        h_s[ic, h] = prod[0:hd, hd:2 * hd]
        qe16[ic, h] = (qd_s[rows, cols] - prod[hd:hd + c, 0:hd]).astype(BF16)
        ou_s[ic, h] = prod[hd:hd + c, hd:2 * hd]

    def phase2_body(ic, carry):
        for h in range(n_heads):
            state = state_ref[h]
            s_b = state.astype(BF16)
            sc16[ic, h] = s_b
            state_ref[h] = (state * gl_s[h, pl.ds(ic, 1), :]
                            + jnp.dot(g16[ic, h], s_b, preferred_element_type=F32)
                            + h_s[ic, h])
        return carry

    lax.fori_loop(0, n_chunks, phase2_body, 0)

    nw = nw_ref[...]
    for i in range(n_items):
        ic, h = items[i]
        rows = rows_of[i]
        o = jnp.dot(qe16[ic, h], sc16[ic, h], preferred_element_type=F32) + ou_s[ic, h]
        ocols = slice(out_col + h * hd, out_col + (h + 1) * hd)
        out_ref[rows, ocols] = (o * _rms_scale(o) * nw * sz_ref[rows, cols_of[i]]).astype(out_ref.dtype)


def _mixer_kernel(x_ref, mod_ref, nw_ref, w_ref, gp_ref,
                  lcw_ref, lcb_ref, lgw_ref, lgb_ref, lam_ref, lnw_ref, gcw_ref, gnw_ref,
                  o_ref,
                  lru_cbuf, hprev_ref, sz_ref, cbuf_q, cbuf_k, cbuf_v, kb16, q16, k16, rhs16,
                  qd_s, kt_s, gcb_s, g16, h_s, qe16, ou_s, sc16, gl_s, state_ref,
                  *, n_heads, lru_width):
    gdn_width = n_heads * GDN_HEAD_DIM
    q_col = 2 * lru_width
    z_col = q_col + 3 * gdn_width
    ba_col = z_col + gdn_width

    @pl.when(pl.program_id(1) == 0)
    def _():
        for cbuf in (lru_cbuf, cbuf_q, cbuf_k, cbuf_v):
            cbuf[0:SUBLANES, :] = jnp.zeros((SUBLANES, cbuf.shape[1]), F32)
        hprev_ref[...] = jnp.zeros_like(hprev_ref)
        state_ref[...] = jnp.zeros_like(state_ref)

    x = x_ref[...]
    h = ((x * _rms_scale(x) * nw_ref[...]) * (1.0 + mod_ref[1]) + mod_ref[0]).astype(BF16)

    def proj(lo, hi):
        return jnp.dot(h, w_ref[:, lo:hi], preferred_element_type=F32)

    x_lru = proj(0, lru_width)
    y_lru = proj(lru_width, q_col)
    q_pre = proj(q_col, q_col + gdn_width)
    k_pre = proj(q_col + gdn_width, q_col + 2 * gdn_width)
    v_pre = proj(q_col + 2 * gdn_width, z_col)
    sz_ref[...] = _silu(proj(z_col, ba_col))
    tail = proj(ba_col, ba_col + LANES)
    lane = lax.broadcasted_iota(jnp.int32, tail.shape, 1)
    g = gp_ref[0:1, :] * _softplus(tail + gp_ref[1:2, :])
    ba = jnp.where(lane < n_heads, _sigmoid(tail), jnp.where(lane < 2 * n_heads, g, 0.0))

    o_ref[:, 0:lru_width] = _lru_branch(
        x_lru, y_lru, lcw_ref, lcb_ref, lgw_ref, lgb_ref, lam_ref, lnw_ref,
        lru_cbuf, hprev_ref).astype(o_ref.dtype)
    _gdn_branch(q_pre, k_pre, v_pre, ba, gcw_ref, gnw_ref, sz_ref, o_ref, lru_width,
                cbuf_q, cbuf_k, cbuf_v, kb16, q16, k16, rhs16, qd_s, kt_s, gcb_s,
                g16, h_s, qe16, ou_s, sc16, gl_s, state_ref, n_heads=n_heads)


def _mixer(x, mod_l, norm_w, w_in_pad, gate_params, lru_conv_w, lru_conv_b, lru_gate_w,
           lru_gate_b, lru_lam, lru_norm_w, gdn_conv_w, gdn_norm_w, *, ts, n_heads):
    bsz, seq, d = x.shape
    lru_width = lru_conv_w.shape[1]
    hd = GDN_HEAD_DIM
    c = GDN_CHUNK
    gdn_width = n_heads * hd
    n_chunks = ts // c
    n_pad = w_in_pad.shape[1]
    out_width = lru_width + gdn_width
    res = functools.partial(_resident, n_grid_axes=2)
    return pl.pallas_call(
        functools.partial(_mixer_kernel, n_heads=n_heads, lru_width=lru_width),
        grid=(bsz, seq // ts),
        in_specs=[
            pl.BlockSpec((None, ts, d), lambda b, t: (b, t, 0)),
            pl.BlockSpec((None, N_MOD, 1, d), lambda b, t: (b, 0, 0, 0)),
            res((1, d)),
            res((d, n_pad)),
            res((SUBLANES, LANES)),
            res((CONV_WIDTH, lru_width)),
            res((1, lru_width)),
            res((lru_width // MXU_DIM, MXU_DIM, 2 * MXU_DIM)),
            res((2, lru_width)),
            res((1, lru_width)),
            res((1, lru_width)),
            res((CONV_WIDTH, 3 * gdn_width)),
            res((1, hd)),
        ],
        out_specs=pl.BlockSpec((None, ts, out_width), lambda b, t: (b, t, 0)),
        out_shape=jax.ShapeDtypeStruct((bsz, seq, out_width), BF16),
        scratch_shapes=[
            pltpu.VMEM((ts + SUBLANES, lru_width), F32),
            pltpu.VMEM((1, lru_width), F32),
            pltpu.VMEM((ts, gdn_width), F32),
            pltpu.VMEM((ts + SUBLANES, gdn_width), F32),
            pltpu.VMEM((ts + SUBLANES, gdn_width), F32),
            pltpu.VMEM((ts + SUBLANES, gdn_width), F32),
            pltpu.VMEM((ts, gdn_width), BF16),
            pltpu.VMEM((ts, gdn_width), BF16),
            pltpu.VMEM((ts, gdn_width), BF16),
            pltpu.VMEM((ts, 2 * gdn_width), BF16),
            pltpu.VMEM((ts, gdn_width), F32),
            pltpu.VMEM((ts, gdn_width), F32),
            pltpu.VMEM((ts, gdn_width), F32),
            pltpu.VMEM((n_chunks, n_heads, hd, hd), BF16),
            pltpu.VMEM((n_chunks, n_heads, hd, hd), F32),
            pltpu.VMEM((n_chunks, n_heads, c, hd), BF16),
            pltpu.VMEM((n_chunks, n_heads, c, hd), F32),
            pltpu.VMEM((n_chunks, n_heads, hd, hd), BF16),
            pltpu.VMEM((n_heads, n_chunks, hd), F32),
            pltpu.VMEM((n_heads, hd, hd), F32),
        ],
        compiler_params=_compiler_params(("arbitrary", "arbitrary")),
        name="mixer",
    )(x, mod_l, norm_w, w_in_pad, gate_params, lru_conv_w, lru_conv_b, lru_gate_w,
      lru_gate_b, lru_lam, lru_norm_w, gdn_conv_w, gdn_norm_w)


def _outmlp_kernel(x_ref, mix_ref, mod_ref, wo_ref, nw_ref, wu_ref, wd_ref,
                   fnw_ref, o_ref, *, final_norm, tf):
    mix = jnp.dot(mix_ref[...], wo_ref[...], preferred_element_type=F32)
    x1 = x_ref[...] + mod_ref[2] * mix
    h2 = ((x1 * _rms_scale(x1) * nw_ref[...]) * (1.0 + mod_ref[4]) + mod_ref[3]).astype(BF16)
    dff = wu_ref.shape[1]
    acc = None
    for j in range(dff // tf):
        up = jnp.dot(h2, wu_ref[:, j * tf:(j + 1) * tf], preferred_element_type=F32)
        act = jnp.square(jnp.maximum(up, 0.0)).astype(BF16)
        down = jnp.dot(act, wd_ref[j * tf:(j + 1) * tf, :], preferred_element_type=F32)
        acc = down if acc is None else acc + down
    y = x1 + mod_ref[5] * acc
    if final_norm:
        y = y * _rms_scale(y) * fnw_ref[...]
    o_ref[...] = y


def _outmlp(x2d, mixed2d, mod_l, w_out, norm_w, w_up, w_down, final_w,
            *, seq, tm, tf, final_norm):
    t, d = x2d.shape
    dff = w_up.shape[1]
    tiles_per_seq = seq // tm
    res = functools.partial(_resident, n_grid_axes=1)
    return pl.pallas_call(
        functools.partial(_outmlp_kernel, final_norm=final_norm, tf=tf),
        grid=(t // tm,),
        in_specs=[
            pl.BlockSpec((tm, d), lambda i: (i, 0)),
            pl.BlockSpec((tm, mixed2d.shape[1]), lambda i: (i, 0)),
            pl.BlockSpec((None, N_MOD, 1, d), lambda i: (i // tiles_per_seq, 0, 0, 0)),
            res((d, d)),
            res((1, d)),
            res((d, dff)),
            res((dff, d)),
            res((1, d)),
        ],
        out_specs=pl.BlockSpec((tm, d), lambda i: (i, 0)),
        out_shape=jax.ShapeDtypeStruct((t, d), F32),
        compiler_params=_compiler_params(("arbitrary",)),
        name="outproj_mlp",
    )(x2d, mixed2d, mod_l, w_out, norm_w, w_up, w_down, final_w)


def _block_diag_gates(gate_a, gate_x):
    g, n, _ = gate_a.shape
    per = MXU_DIM // n
    n_blk = g // per

    def diag(w):
        w = w.reshape(n_blk, per, n, n)
        eye = jnp.eye(per, dtype=w.dtype)
        return jnp.einsum("bpij,pq->bpiqj", w, eye).reshape(n_blk, MXU_DIM, MXU_DIM)

    return jnp.concatenate([diag(gate_a), diag(gate_x)], axis=2)


def kernel(x, c, norm_mix_w, norm_mlp_w, w_mod, b_mod, w_in, lru_conv_w, lru_conv_b,
           lru_gate_a_w, lru_gate_a_b, lru_gate_x_w, lru_gate_x_b, lru_lambda, lru_norm_w,
           gdn_conv_w, gdn_a_log, gdn_dt_bias, gdn_norm_w, w_out, w_up, w_down, final_norm_w):
    bsz, seq, d = x.shape
    depth = w_in.shape[0]
    lru_width = lru_conv_w.shape[2]
    n_heads = gdn_a_log.shape[1]
    gdn_width = n_heads * GDN_HEAD_DIM
    in_cols = w_in.shape[2]
    ba_col = 2 * lru_width + 4 * gdn_width
    assert in_cols == ba_col + 2 * n_heads
    n_pad = ba_col + LANES

    mod = _modulation(c, w_mod, b_mod)
    w_in_pad = jnp.zeros((depth, d, n_pad), BF16).at[:, :, :in_cols].set(w_in.astype(BF16))
    w_out_b = w_out.astype(BF16)
    w_up_b = w_up.astype(BF16)
    w_down_b = w_down.astype(BF16)
    gate_params = jnp.zeros((depth, SUBLANES, LANES), F32)
    gate_params = gate_params.at[:, 0, n_heads:2 * n_heads].set(-jnp.exp(gdn_a_log))
    gate_params = gate_params.at[:, 1, n_heads:2 * n_heads].set(gdn_dt_bias)

    for l in range(depth):
        gate_w = _block_diag_gates(lru_gate_a_w[l], lru_gate_x_w[l]).astype(BF16)
        gate_b = jnp.stack([lru_gate_a_b[l], lru_gate_x_b[l]])
        mixed = _mixer(x, mod[l], norm_mix_w[l][None], w_in_pad[l], gate_params[l],
                       lru_conv_w[l], lru_conv_b[l][None], gate_w, gate_b,
                       lru_lambda[l][None], lru_norm_w[l][None],
                       gdn_conv_w[l], gdn_norm_w[l][None], ts=MIXER_ROWS, n_heads=n_heads)
        x2d = _outmlp(x.reshape(bsz * seq, d), mixed.reshape(bsz * seq, lru_width + gdn_width),
                      mod[l], w_out_b[l], norm_mlp_w[l][None], w_up_b[l], w_down_b[l],
                      final_norm_w[None], seq=seq, tm=MLP_ROWS, tf=MLP_FF_CHUNK,
                      final_norm=(l == depth - 1))
        x = x2d.reshape(bsz, seq, d)
    return x
```

```python
import functools
import math

import jax
import jax.numpy as jnp
from jax import lax
from jax.experimental import pallas as pl
from jax.experimental.pallas import tpu as pltpu

LRU_C = 8.0
GDN_HEAD_DIM = 128
GDN_CHUNK = 64
CONV_WIDTH = 4
N_MOD = 6
NORM_EPS = 1e-6

LANES = 128
SUBLANES = 8
MXU_DIM = 256
VMEM_LIMIT_BYTES = 56 * 1024 * 1024

MIXER_ROWS = 512
MLP_ROWS = 512
MLP_FF_CHUNK = 1024

F32 = jnp.float32
BF16 = jnp.bfloat16


def _compiler_params(semantics):
    return pltpu.CompilerParams(dimension_semantics=semantics,
                                vmem_limit_bytes=VMEM_LIMIT_BYTES)


def _resident(shape, n_grid_axes):
    zeros = (0,) * len(shape)
    if n_grid_axes == 1:
        index_map = lambda i: zeros
    else:
        index_map = lambda i, j: zeros
    return pl.BlockSpec(shape, index_map, pipeline_mode=pl.Buffered(1))


def _sigmoid(x):
    return 0.5 * jnp.tanh(0.5 * x) + 0.5


def _silu(x):
    half = 0.5 * x
    return half * jnp.tanh(half) + half


def _softplus(x):
    return jnp.maximum(x, 0.0) + jnp.log1p(jnp.exp(-jnp.abs(x)))


def _rms_scale(x):
    return lax.rsqrt(jnp.mean(x * x, axis=-1, keepdims=True) + NORM_EPS)


def _l2norm(t):
    return t * lax.rsqrt(jnp.sum(t * t, axis=-1, keepdims=True) + 1e-6)


def _mod_kernel(c_ref, w_ref, b_ref, o_ref):
    c_act = _silu(c_ref[...]).astype(BF16)
    o_ref[...] = jnp.dot(c_act, w_ref[...].astype(BF16),
                         preferred_element_type=F32) + b_ref[...]


def _modulation(c, w_mod, b_mod):
    depth, d, n = w_mod.shape
    b = c.shape[0]
    bp = max(SUBLANES, b)
    c_pad = jnp.zeros((bp, d), F32).at[:b].set(c)
    tn = n // N_MOD
    out = pl.pallas_call(
        _mod_kernel,
        grid=(depth, n // tn),
        in_specs=[
            pl.BlockSpec((bp, d), lambda l, j: (0, 0)),
            pl.BlockSpec((None, d, tn), lambda l, j: (l, 0, j)),
            pl.BlockSpec((None, 1, tn), lambda l, j: (l, 0, j)),
        ],
        out_specs=pl.BlockSpec((None, bp, tn), lambda l, j: (l, 0, j)),
        out_shape=jax.ShapeDtypeStruct((depth, bp, n), F32),
        compiler_params=_compiler_params(("arbitrary", "arbitrary")),
        name="modulation",
    )(c_pad, w_mod, b_mod.reshape(depth, 1, n))
    return out[:, :b].reshape(depth, b, N_MOD, 1, d)


def _causal_conv(hist_ref, x, w_ref):
    assert w_ref.shape[0] == CONV_WIDTH == 4
    ts = x.shape[0]

    def shifted(hist, val, k):
        cat = jnp.concatenate([hist, val], axis=0)
        return pltpu.roll(cat, k, axis=0)[SUBLANES:]

    x_prev = shifted(hist_ref[0], x, 1)
    v = x * w_ref[1:2, :] + x_prev * w_ref[0:1, :]
    y = x * w_ref[3:4, :] + x_prev * w_ref[2:3, :] + shifted(hist_ref[1], v, 2)
    hist_ref[0] = x[ts - SUBLANES:]
    hist_ref[1] = v[ts - SUBLANES:]
    return y


def _lru_branch(x_lru, y_lru, cw_ref, cb_ref, gw_ref, gb_ref, lam_ref, nw_ref,
                cbuf_ref, hprev_ref):
    ts, width = x_lru.shape
    xr = _causal_conv(cbuf_ref, x_lru, cw_ref) + cb_ref[...]

    xr_b = xr.astype(BF16)
    r_parts, i_parts = [], []
    for blk in range(width // MXU_DIM):
        cols = slice(blk * MXU_DIM, (blk + 1) * MXU_DIM)
        gates = jnp.dot(xr_b[:, cols], gw_ref[blk], preferred_element_type=F32)
        r_parts.append(gates[:, :MXU_DIM])
        i_parts.append(gates[:, MXU_DIM:])
    r = _sigmoid(jnp.concatenate(r_parts, axis=1) + gb_ref[0:1, :])
    i = _sigmoid(jnp.concatenate(i_parts, axis=1) + gb_ref[1:2, :])
    lam = lam_ref[...]
    log_sig_lam = jnp.minimum(lam, 0.0) - jnp.log1p(jnp.exp(-jnp.abs(lam)))
    log_a = LRU_C * r * log_sig_lam
    a = jnp.exp(log_a)
    mult = jnp.sqrt(jnp.maximum(1.0 - jnp.exp(2.0 * log_a), 1e-12))
    b = mult * (i * xr)

    sub = lax.broadcasted_iota(jnp.int32, (ts, width), 0) & (SUBLANES - 1)
    shift = 1
    while shift < SUBLANES:
        valid = sub >= shift
        a_sh = pltpu.roll(a, shift, axis=0)
        b_sh = pltpu.roll(b, shift, axis=0)
        b = b + jnp.where(valid, a * b_sh, 0.0)
        a = jnp.where(valid, a * a_sh, a)
        shift *= 2
    carry = hprev_ref[...]
    groups = []
    for r in range(ts // SUBLANES):
        rows = slice(r * SUBLANES, (r + 1) * SUBLANES)
        h_r = a[rows] * carry + b[rows]
        groups.append(h_r)
        carry = h_r[SUBLANES - 1:SUBLANES, :]
    b = jnp.concatenate(groups, axis=0)
    hprev_ref[...] = carry

    y3 = y_lru * y_lru * y_lru
    gelu = 0.5 * y_lru * (1.0 + jnp.tanh(math.sqrt(2.0 / math.pi) * (y_lru + 0.044715 * y3)))
    m = b * gelu
    return m * _rms_scale(m) * nw_ref[...]


def _gdn_branch(q_pre, k_pre, v_pre, ba, cw_ref, nw_ref, sz_ref, out_ref, out_col,
                cbuf_q, cbuf_k, cbuf_v, kb16, q16, k16, rhs16, qd_s, kt_s, gcb_s,
                g16, h_s, qe16, ou_s, sc16, gl_s, state_ref, *, n_heads):
    ts = q_pre.shape[0]
    hd = GDN_HEAD_DIM
    c = GDN_CHUNK
    n_chunks = ts // c
    width = n_heads * hd

    ri = lax.broadcasted_iota(jnp.int32, (c, c), 0)
    ci = lax.broadcasted_iota(jnp.int32, (c, c), 1)
    tril = (ri >= ci).astype(F32)
    causal = ri >= ci
    strict = ri > ci
    eye = (ri == ci).astype(F32)
    pair_mask = (ri == ci + 1) & ((ri & 1) == 1)
    merge_masks = []
    sz = 2
    while sz < c:
        sh = sz.bit_length() - 1
        rb = ri >> sh
        merge_masks.append((rb == (ci >> sh) + 1) & ((rb & 1) == 1))
        sz *= 2

    qc = _silu(_causal_conv(cbuf_q, q_pre, cw_ref.at[:, 0:width]))
    kc = _silu(_causal_conv(cbuf_k, k_pre, cw_ref.at[:, width:2 * width]))
    vc = _silu(_causal_conv(cbuf_v, v_pre, cw_ref.at[:, 2 * width:3 * width]))
    gc_all = jnp.concatenate(
        [jnp.dot(tril, ba[ic * c:(ic + 1) * c], precision=lax.Precision.HIGHEST,
                 preferred_element_type=F32) for ic in range(n_chunks)], axis=0)
    for h in range(n_heads):
        cols = slice(h * hd, (h + 1) * hd)
        beta = jnp.broadcast_to(ba[:, h:h + 1], (ts, hd))
        gcb = jnp.broadcast_to(gc_all[:, n_heads + h:n_heads + h + 1], (ts, hd))
        gcb3 = gcb.reshape(n_chunks, c, hd)
        g_last = gcb3[:, c - 1:c, :]
        eg = jnp.exp(gcb)
        kt_scale = jnp.exp(g_last - gcb3).reshape(ts, hd)
        qn = _l2norm(qc[:, cols]) * (hd ** -0.5)
        kn = _l2norm(kc[:, cols])
        kb = kn * beta
        kb16[:, cols] = kb.astype(BF16)
        q16[:, cols] = qn.astype(BF16)
        k16[:, cols] = kn.astype(BF16)
        rhs16[:, 2 * h * hd:(2 * h + 1) * hd] = (vc[:, cols] * beta).astype(BF16)
        rhs16[:, (2 * h + 1) * hd:(2 * h + 2) * hd] = (kb * eg).astype(BF16)
        qd_s[:, cols] = qn * eg
        kt_s[:, cols] = kn * kt_scale
        gcb_s[:, cols] = gcb
        gl_s[h] = jnp.exp(g_last).reshape(n_chunks, hd)

    items = [(ic, h) for ic in range(n_chunks) for h in range(n_heads)]
    rows_of = [slice(ic * c, (ic + 1) * c) for ic, _ in items]
    cols_of = [slice(h * hd, (h + 1) * hd) for _, h in items]
    n_items = len(items)
    lmats, attns, ps = [], [], []
    for i in range(n_items):
        rows, cols = rows_of[i], cols_of[i]
        gcb = gcb_s[rows, cols]
        diff = gcb[:, 0:c] - gcb.T[0:c, :]
        decay = jnp.where(causal, jnp.exp(jnp.where(causal, diff, 0.0)), 0.0)
        a_lhs = jnp.concatenate([kb16[rows, cols], q16[rows, cols]], axis=0)
        kkqk = lax.dot_general(a_lhs, k16[rows, cols], (((1,), (1,)), ((), ())),
                               preferred_element_type=F32)
        lmat = jnp.where(strict, kkqk[0:c] * decay, 0.0)
        lmats.append(lmat)
        attns.append(jnp.where(causal, kkqk[c:2 * c] * decay, 0.0).astype(BF16))
        ps.append(eye - jnp.where(pair_mask, lmat, 0.0))
    for merge_mask in merge_masks:
        p_bs = [p.astype(BF16) for p in ps]
        cps = [jnp.dot(jnp.where(merge_mask, lmats[i], 0.0).astype(BF16), p_bs[i],
                       preferred_element_type=F32) for i in range(n_items)]
        ps = [ps[i] - jnp.dot(p_bs[i], cps[i].astype(BF16), preferred_element_type=F32)
              for i in range(n_items)]
    wus = []
    for i in range(n_items):
        h = items[i][1]
        uw = jnp.dot(ps[i].astype(BF16), rhs16[rows_of[i], 2 * h * hd:(2 * h + 2) * hd],
                     preferred_element_type=F32)
        wus.append(jnp.concatenate([uw[:, hd:2 * hd], uw[:, 0:hd]], axis=1).astype(BF16))
    for i in range(n_items):
        ic, h = items[i]
        rows, cols = rows_of[i], cols_of[i]
        lhs = jnp.concatenate([kt_s[rows, cols].T.astype(BF16), attns[i]], axis=0)
        prod = jnp.dot(lhs, wus[i], preferred_element_type=F32)
        g16[ic, h] = (-prod[0:hd, 0:hd]).astype(BF16)
        h_s[ic, h] = prod[0:hd, hd:2 * hd]
        qe16[ic, h] = (qd_s[rows, cols] - prod[hd:hd + c, 0:hd]).astype(BF16)
        ou_s[ic, h] = prod[hd:hd + c, hd:2 * hd]

    def phase2_body(ic, carry):
        for h in range(n_heads):
            state = state_ref[h]
            s_b = state.astype(BF16)
            sc16[ic, h] = s_b
            state_ref[h] = (state * gl_s[h, pl.ds(ic, 1), :]
                            + jnp.dot(g16[ic, h], s_b, preferred_element_type=F32)
                            + h_s[ic, h])
        return carry

    lax.fori_loop(0, n_chunks, phase2_body, 0)

    nw = nw_ref[...]
    for i in range(n_items):
        ic, h = items[i]
        rows = rows_of[i]
        o = jnp.dot(qe16[ic, h], sc16[ic, h], preferred_element_type=F32) + ou_s[ic, h]
        ocols = slice(out_col + h * hd, out_col + (h + 1) * hd)
        out_ref[rows, ocols] = (o * _rms_scale(o) * nw * sz_ref[rows, cols_of[i]]).astype(out_ref.dtype)


def _mixer_kernel(x_ref, mod_ref, nw_ref, w_ref, gp_ref,
                  lcw_ref, lcb_ref, lgw_ref, lgb_ref, lam_ref, lnw_ref, gcw_ref, gnw_ref,
                  o_ref,
                  lru_cbuf, hprev_ref, sz_ref, cbuf_q, cbuf_k, cbuf_v, kb16, q16, k16, rhs16,
                  qd_s, kt_s, gcb_s, g16, h_s, qe16, ou_s, sc16, gl_s, state_ref,
                  *, n_heads, lru_width):
    gdn_width = n_heads * GDN_HEAD_DIM
    q_col = 2 * lru_width
    z_col = q_col + 3 * gdn_width
    ba_col = z_col + gdn_width

    @pl.when(pl.program_id(1) == 0)
    def _():
        for hist in (lru_cbuf, cbuf_q, cbuf_k, cbuf_v):
            hist[...] = jnp.zeros_like(hist)
        hprev_ref[...] = jnp.zeros_like(hprev_ref)
        state_ref[...] = jnp.zeros_like(state_ref)

    x = x_ref[...]
    h = ((x * _rms_scale(x) * nw_ref[...]) * (1.0 + mod_ref[1]) + mod_ref[0]).astype(BF16)

    def proj(lo, hi):
        return jnp.dot(h, w_ref[:, lo:hi], preferred_element_type=F32)

    x_lru = proj(0, lru_width)
    y_lru = proj(lru_width, q_col)
    q_pre = proj(q_col, q_col + gdn_width)
    k_pre = proj(q_col + gdn_width, q_col + 2 * gdn_width)
    v_pre = proj(q_col + 2 * gdn_width, z_col)
    sz_ref[...] = _silu(proj(z_col, ba_col))
    tail = proj(ba_col, ba_col + LANES)
    lane = lax.broadcasted_iota(jnp.int32, tail.shape, 1)
    g = gp_ref[0:1, :] * _softplus(tail + gp_ref[1:2, :])
    ba = jnp.where(lane < n_heads, _sigmoid(tail), jnp.where(lane < 2 * n_heads, g, 0.0))

    o_ref[:, 0:lru_width] = _lru_branch(
        x_lru, y_lru, lcw_ref, lcb_ref, lgw_ref, lgb_ref, lam_ref, lnw_ref,
        lru_cbuf, hprev_ref).astype(o_ref.dtype)
    _gdn_branch(q_pre, k_pre, v_pre, ba, gcw_ref, gnw_ref, sz_ref, o_ref, lru_width,
                cbuf_q, cbuf_k, cbuf_v, kb16, q16, k16, rhs16, qd_s, kt_s, gcb_s,
                g16, h_s, qe16, ou_s, sc16, gl_s, state_ref, n_heads=n_heads)


def _mixer(x, mod_l, norm_w, w_in_pad, gate_params, lru_conv_w, lru_conv_b, lru_gate_w,
           lru_gate_b, lru_lam, lru_norm_w, gdn_conv_w, gdn_norm_w, *, ts, n_heads):
    bsz, seq, d = x.shape
    lru_width = lru_conv_w.shape[1]
    hd = GDN_HEAD_DIM
    c = GDN_CHUNK
    gdn_width = n_heads * hd
    n_chunks = ts // c
    n_pad = w_in_pad.shape[1]
    out_width = lru_width + gdn_width
    res = functools.partial(_resident, n_grid_axes=2)
    return pl.pallas_call(
        functools.partial(_mixer_kernel, n_heads=n_heads, lru_width=lru_width),
        grid=(bsz, seq // ts),
        in_specs=[
            pl.BlockSpec((None, ts, d), lambda b, t: (b, t, 0)),
            pl.BlockSpec((None, N_MOD, 1, d), lambda b, t: (b, 0, 0, 0)),
            res((1, d)),
            res((d, n_pad)),
            res((SUBLANES, LANES)),
            res((CONV_WIDTH, lru_width)),
            res((1, lru_width)),
            res((lru_width // MXU_DIM, MXU_DIM, 2 * MXU_DIM)),
            res((2, lru_width)),
            res((1, lru_width)),
            res((1, lru_width)),
            res((CONV_WIDTH, 3 * gdn_width)),
            res((1, hd)),
        ],
        out_specs=pl.BlockSpec((None, ts, out_width), lambda b, t: (b, t, 0)),
        out_shape=jax.ShapeDtypeStruct((bsz, seq, out_width), BF16),
        scratch_shapes=[
            pltpu.VMEM((2, SUBLANES, lru_width), F32),
            pltpu.VMEM((1, lru_width), F32),
            pltpu.VMEM((ts, gdn_width), F32),
            pltpu.VMEM((2, SUBLANES, gdn_width), F32),
            pltpu.VMEM((2, SUBLANES, gdn_width), F32),
            pltpu.VMEM((2, SUBLANES, gdn_width), F32),
            pltpu.VMEM((ts, gdn_width), BF16),
            pltpu.VMEM((ts, gdn_width), BF16),
            pltpu.VMEM((ts, gdn_width), BF16),
            pltpu.VMEM((ts, 2 * gdn_width), BF16),
            pltpu.VMEM((ts, gdn_width), F32),
            pltpu.VMEM((ts, gdn_width), F32),
            pltpu.VMEM((ts, gdn_width), F32),
            pltpu.VMEM((n_chunks, n_heads, hd, hd), BF16),
            pltpu.VMEM((n_chunks, n_heads, hd, hd), F32),
            pltpu.VMEM((n_chunks, n_heads, c, hd), BF16),
            pltpu.VMEM((n_chunks, n_heads, c, hd), F32),
            pltpu.VMEM((n_chunks, n_heads, hd, hd), BF16),
            pltpu.VMEM((n_heads, n_chunks, hd), F32),
            pltpu.VMEM((n_heads, hd, hd), F32),
        ],
        compiler_params=_compiler_params(("arbitrary", "arbitrary")),
        name="mixer",
    )(x, mod_l, norm_w, w_in_pad, gate_params, lru_conv_w, lru_conv_b, lru_gate_w,
      lru_gate_b, lru_lam, lru_norm_w, gdn_conv_w, gdn_norm_w)


def _outmlp_kernel(x_ref, mix_ref, mod_ref, wo_ref, nw_ref, wu_ref, wd_ref,
                   fnw_ref, o_ref, *, final_norm, tf):
    mix = jnp.dot(mix_ref[...], wo_ref[...], preferred_element_type=F32)
    x1 = x_ref[...] + mod_ref[2] * mix
    h2 = ((x1 * _rms_scale(x1) * nw_ref[...]) * (1.0 + mod_ref[4]) + mod_ref[3]).astype(BF16)
    dff = wu_ref.shape[1]
    acc = None
    for j in range(dff // tf):
        up = jnp.dot(h2, wu_ref[:, j * tf:(j + 1) * tf], preferred_element_type=F32)
        act = jnp.square(jnp.maximum(up, 0.0)).astype(BF16)
        down = jnp.dot(act, wd_ref[j * tf:(j + 1) * tf, :], preferred_element_type=F32)
        acc = down if acc is None else acc + down
    y = x1 + mod_ref[5] * acc
    if final_norm:
        y = y * _rms_scale(y) * fnw_ref[...]
    o_ref[...] = y


def _outmlp(x2d, mixed2d, mod_l, w_out, norm_w, w_up, w_down, final_w,
            *, seq, tm, tf, final_norm):
    t, d = x2d.shape
    dff = w_up.shape[1]
    tiles_per_seq = seq // tm
    res = functools.partial(_resident, n_grid_axes=1)
    return pl.pallas_call(
        functools.partial(_outmlp_kernel, final_norm=final_norm, tf=tf),
        grid=(t // tm,),
        in_specs=[
            pl.BlockSpec((tm, d), lambda i: (i, 0)),
            pl.BlockSpec((tm, mixed2d.shape[1]), lambda i: (i, 0)),
            pl.BlockSpec((None, N_MOD, 1, d), lambda i: (i // tiles_per_seq, 0, 0, 0)),
            res((d, d)),
            res((1, d)),
            res((d, dff)),
            res((dff, d)),
            res((1, d)),
        ],
        out_specs=pl.BlockSpec((tm, d), lambda i: (i, 0)),
        out_shape=jax.ShapeDtypeStruct((t, d), F32),
        compiler_params=_compiler_params(("arbitrary",)),
        name="outproj_mlp",
    )(x2d, mixed2d, mod_l, w_out, norm_w, w_up, w_down, final_w)


def _block_diag_gates(gate_a, gate_x):
    g, n, _ = gate_a.shape
    per = MXU_DIM // n
    n_blk = g // per

    def diag(w):
        w = w.reshape(n_blk, per, n, n)
        eye = jnp.eye(per, dtype=w.dtype)
        return jnp.einsum("bpij,pq->bpiqj", w, eye).reshape(n_blk, MXU_DIM, MXU_DIM)

    return jnp.concatenate([diag(gate_a), diag(gate_x)], axis=2)


def kernel(x, c, norm_mix_w, norm_mlp_w, w_mod, b_mod, w_in, lru_conv_w, lru_conv_b,
           lru_gate_a_w, lru_gate_a_b, lru_gate_x_w, lru_gate_x_b, lru_lambda, lru_norm_w,
           gdn_conv_w, gdn_a_log, gdn_dt_bias, gdn_norm_w, w_out, w_up, w_down, final_norm_w):
    bsz, seq, d = x.shape
    depth = w_in.shape[0]
    lru_width = lru_conv_w.shape[2]
    n_heads = gdn_a_log.shape[1]
    gdn_width = n_heads * GDN_HEAD_DIM
    in_cols = w_in.shape[2]
    ba_col = 2 * lru_width + 4 * gdn_width
    assert in_cols == ba_col + 2 * n_heads
    n_pad = ba_col + LANES

    mod = _modulation(c, w_mod, b_mod)
    w_in_pad = jnp.zeros((depth, d, n_pad), BF16).at[:, :, :in_cols].set(w_in.astype(BF16))
    w_out_b = w_out.astype(BF16)
    w_up_b = w_up.astype(BF16)
    w_down_b = w_down.astype(BF16)
    gate_params = jnp.zeros((depth, SUBLANES, LANES), F32)
    gate_params = gate_params.at[:, 0, n_heads:2 * n_heads].set(-jnp.exp(gdn_a_log))
    gate_params = gate_params.at[:, 1, n_heads:2 * n_heads].set(gdn_dt_bias)

    for l in range(depth):
        gate_w = _block_diag_gates(lru_gate_a_w[l], lru_gate_x_w[l]).astype(BF16)
        gate_b = jnp.stack([lru_gate_a_b[l], lru_gate_x_b[l]])
        mixed = _mixer(x, mod[l], norm_mix_w[l][None], w_in_pad[l], gate_params[l],
                       lru_conv_w[l], lru_conv_b[l][None], gate_w, gate_b,
                       lru_lambda[l][None], lru_norm_w[l][None],
                       gdn_conv_w[l], gdn_norm_w[l][None], ts=MIXER_ROWS, n_heads=n_heads)
        x2d = _outmlp(x.reshape(bsz * seq, d), mixed.reshape(bsz * seq, lru_width + gdn_width),
                      mod[l], w_out_b[l], norm_mlp_w[l][None], w_up_b[l], w_down_b[l],
                      final_norm_w[None], seq=seq, tm=MLP_ROWS, tf=MLP_FF_CHUNK,
                      final_norm=(l == depth - 1))
        x = x2d.reshape(bsz, seq, d)
    return x
```

```python
import functools
import math

import jax
import jax.numpy as jnp
from jax import lax
from jax.experimental import pallas as pl
from jax.experimental.pallas import tpu as pltpu

LRU_C = 8.0
GDN_HEAD_DIM = 128
GDN_CHUNK = 64
CONV_WIDTH = 4
N_MOD = 6
NORM_EPS = 1e-6

LANES = 128
SUBLANES = 8
MXU_DIM = 256
VMEM_LIMIT_BYTES = 56 * 1024 * 1024

MIXER_ROWS = 512
MLP_ROWS = 512
MLP_FF_CHUNK = 1024

F32 = jnp.float32
BF16 = jnp.bfloat16


def _compiler_params(semantics):
    return pltpu.CompilerParams(dimension_semantics=semantics,
                                vmem_limit_bytes=VMEM_LIMIT_BYTES)


def _resident(shape, n_grid_axes, layer=None):
    index = (0,) * len(shape)
    if layer is not None:
        shape = (None,) + tuple(shape)
        index = (layer,) + index
    if n_grid_axes == 1:
        index_map = lambda i: index
    else:
        index_map = lambda i, j: index
    return pl.BlockSpec(shape, index_map, pipeline_mode=pl.Buffered(1))


def _sigmoid(x):
    return 0.5 * jnp.tanh(0.5 * x) + 0.5


def _silu(x):
    half = 0.5 * x
    return half * jnp.tanh(half) + half


def _softplus(x):
    return jnp.maximum(x, 0.0) + jnp.log1p(jnp.exp(-jnp.abs(x)))


def _rms_scale(x):
    return lax.rsqrt(jnp.mean(x * x, axis=-1, keepdims=True) + NORM_EPS)


def _l2norm(t):
    return t * lax.rsqrt(jnp.sum(t * t, axis=-1, keepdims=True) + 1e-6)


def _mod_kernel(c_ref, w_ref, b_ref, o_ref):
    c_act = _silu(c_ref[...]).astype(BF16)
    o_ref[...] = jnp.dot(c_act, w_ref[...].astype(BF16),
                         preferred_element_type=F32) + b_ref[...]


def _modulation(c, w_mod, b_mod):
    depth, d, n = w_mod.shape
    b = c.shape[0]
    bp = max(SUBLANES, b)
    c_pad = jnp.zeros((bp, d), F32).at[:b].set(c)
    tn = n // N_MOD
    out = pl.pallas_call(
        _mod_kernel,
        grid=(depth, n // tn),
        in_specs=[
            pl.BlockSpec((bp, d), lambda l, j: (0, 0)),
            pl.BlockSpec((None, d, tn), lambda l, j: (l, 0, j)),
            pl.BlockSpec((None, 1, tn), lambda l, j: (l, 0, j)),
        ],
        out_specs=pl.BlockSpec((None, bp, tn), lambda l, j: (l, 0, j)),
        out_shape=jax.ShapeDtypeStruct((depth, bp, n), F32),
        compiler_params=_compiler_params(("arbitrary", "arbitrary")),
        name="modulation",
    )(c_pad, w_mod, b_mod.reshape(depth, 1, n))
    return out[:, :b].reshape(depth, b, N_MOD, 1, d)


def _causal_conv(hist_ref, x, w_ref):
    assert w_ref.shape[0] == CONV_WIDTH == 4
    ts = x.shape[0]

    def shifted(hist, val, k):
        cat = jnp.concatenate([hist, val], axis=0)
        return pltpu.roll(cat, k, axis=0)[SUBLANES:]

    x_prev = shifted(hist_ref[0], x, 1)
    v = x * w_ref[1:2, :] + x_prev * w_ref[0:1, :]
    y = x * w_ref[3:4, :] + x_prev * w_ref[2:3, :] + shifted(hist_ref[1], v, 2)
    hist_ref[0] = x[ts - SUBLANES:]
    hist_ref[1] = v[ts - SUBLANES:]
    return y


def _lru_branch(x_lru, y_lru, cw_ref, cb_ref, gw_ref, gb_ref, lam_ref, nw_ref,
                cbuf_ref, hprev_ref):
    ts, width = x_lru.shape
    xr = _causal_conv(cbuf_ref, x_lru, cw_ref) + cb_ref[...]

    xr_b = xr.astype(BF16)
    r_parts, i_parts = [], []
    for blk in range(width // MXU_DIM):
        cols = slice(blk * MXU_DIM, (blk + 1) * MXU_DIM)
        gates = jnp.dot(xr_b[:, cols], gw_ref[blk], preferred_element_type=F32)
        r_parts.append(gates[:, :MXU_DIM])
        i_parts.append(gates[:, MXU_DIM:])
    r = _sigmoid(jnp.concatenate(r_parts, axis=1) + gb_ref[0:1, :])
    i = _sigmoid(jnp.concatenate(i_parts, axis=1) + gb_ref[1:2, :])
    lam = lam_ref[...]
    log_sig_lam = jnp.minimum(lam, 0.0) - jnp.log1p(jnp.exp(-jnp.abs(lam)))
    log_a = LRU_C * r * log_sig_lam
    a = jnp.exp(log_a)
    mult = jnp.sqrt(jnp.maximum(1.0 - a * a, 1e-12))
    b = mult * (i * xr)

    sub = lax.broadcasted_iota(jnp.int32, (ts, width), 0) & (SUBLANES - 1)
    shift = 1
    while shift < SUBLANES:
        valid = sub >= shift
        a_sh = pltpu.roll(a, shift, axis=0)
        b_sh = pltpu.roll(b, shift, axis=0)
        b = b + jnp.where(valid, a * b_sh, 0.0)
        a = jnp.where(valid, a * a_sh, a)
        shift *= 2
    carry = hprev_ref[...]
    groups = []
    for r in range(ts // SUBLANES):
        rows = slice(r * SUBLANES, (r + 1) * SUBLANES)
        h_r = a[rows] * carry + b[rows]
        groups.append(h_r)
        carry = h_r[SUBLANES - 1:SUBLANES, :]
    b = jnp.concatenate(groups, axis=0)
    hprev_ref[...] = carry

    y3 = y_lru * y_lru * y_lru
    gelu = 0.5 * y_lru * (1.0 + jnp.tanh(math.sqrt(2.0 / math.pi) * (y_lru + 0.044715 * y3)))
    m = b * gelu
    return m * _rms_scale(m) * nw_ref[...]


def _gdn_branch(q_pre, k_pre, v_pre, ba, cw_ref, nw_ref, sz_ref, out_ref, out_col,
                cbuf_q, cbuf_k, cbuf_v, kb16, q16, k16, rhs16, qd_s, kt_s, gcb_s,
                g16, h_s, qe16, ou_s, sc16, gl_s, state_ref, *, n_heads):
    ts = q_pre.shape[0]
    hd = GDN_HEAD_DIM
    c = GDN_CHUNK
    n_chunks = ts // c
    width = n_heads * hd

    ri = lax.broadcasted_iota(jnp.int32, (c, c), 0)
    ci = lax.broadcasted_iota(jnp.int32, (c, c), 1)
    tril = (ri >= ci).astype(F32)
    causal = ri >= ci
    strict = ri > ci
    eye = (ri == ci).astype(F32)
    pair_mask = (ri == ci + 1) & ((ri & 1) == 1)
    merge_masks = []
    sz = 2
    while sz < c:
        sh = sz.bit_length() - 1
        rb = ri >> sh
        merge_masks.append((rb == (ci >> sh) + 1) & ((rb & 1) == 1))
        sz *= 2

    qc = _silu(_causal_conv(cbuf_q, q_pre, cw_ref.at[:, 0:width]))
    kc = _silu(_causal_conv(cbuf_k, k_pre, cw_ref.at[:, width:2 * width]))
    vc = _silu(_causal_conv(cbuf_v, v_pre, cw_ref.at[:, 2 * width:3 * width]))
    gc_all = jnp.concatenate(
        [jnp.dot(tril, ba[ic * c:(ic + 1) * c], precision=lax.Precision.HIGHEST,
                 preferred_element_type=F32) for ic in range(n_chunks)], axis=0)
    for h in range(n_heads):
        cols = slice(h * hd, (h + 1) * hd)
        beta = jnp.broadcast_to(ba[:, h:h + 1], (ts, hd))
        gcb = jnp.broadcast_to(gc_all[:, n_heads + h:n_heads + h + 1], (ts, hd))
        gcb3 = gcb.reshape(n_chunks, c, hd)
        g_last = gcb3[:, c - 1:c, :]
        eg = jnp.exp(gcb)
        kt_scale = jnp.exp(g_last - gcb3).reshape(ts, hd)
        qn = _l2norm(qc[:, cols]) * (hd ** -0.5)
        kn = _l2norm(kc[:, cols])
        kb = kn * beta
        kb16[:, cols] = kb.astype(BF16)
        q16[:, cols] = qn.astype(BF16)
        k16[:, cols] = kn.astype(BF16)
        rhs16[:, 2 * h * hd:(2 * h + 1) * hd] = (vc[:, cols] * beta).astype(BF16)
        rhs16[:, (2 * h + 1) * hd:(2 * h + 2) * hd] = (kb * eg).astype(BF16)
        qd_s[:, cols] = qn * eg
        kt_s[:, cols] = kn * kt_scale
        gcb_s[:, cols] = gcb
        gl_s[h] = jnp.exp(g_last).reshape(n_chunks, hd)

    items = [(ic, h) for ic in range(n_chunks) for h in range(n_heads)]
    rows_of = [slice(ic * c, (ic + 1) * c) for ic, _ in items]
    cols_of = [slice(h * hd, (h + 1) * hd) for _, h in items]
    n_items = len(items)
    lmats, attns, ps = [], [], []
    for i in range(n_items):
        rows, cols = rows_of[i], cols_of[i]
        gcb = gcb_s[rows, cols]
        diff = gcb[:, 0:c] - gcb.T[0:c, :]
        decay = jnp.where(causal, jnp.exp(jnp.where(causal, diff, 0.0)), 0.0)
        a_lhs = jnp.concatenate([kb16[rows, cols], q16[rows, cols]], axis=0)
        kkqk = lax.dot_general(a_lhs, k16[rows, cols], (((1,), (1,)), ((), ())),
                               preferred_element_type=F32)
        lmat = jnp.where(strict, kkqk[0:c] * decay, 0.0)
        lmats.append(lmat)
        attns.append(jnp.where(causal, kkqk[c:2 * c] * decay, 0.0).astype(BF16))
        ps.append(eye - jnp.where(pair_mask, lmat, 0.0))
    for merge_mask in merge_masks:
        p_bs = [p.astype(BF16) for p in ps]
        cps = [jnp.dot(jnp.where(merge_mask, lmats[i], 0.0).astype(BF16), p_bs[i],
                       preferred_element_type=F32) for i in range(n_items)]
        ps = [ps[i] - jnp.dot(p_bs[i], cps[i].astype(BF16), preferred_element_type=F32)
              for i in range(n_items)]
    wus = []
    for i in range(n_items):
        h = items[i][1]
        uw = jnp.dot(ps[i].astype(BF16), rhs16[rows_of[i], 2 * h * hd:(2 * h + 2) * hd],
                     preferred_element_type=F32)
        wus.append(jnp.concatenate([uw[:, hd:2 * hd], uw[:, 0:hd]], axis=1).astype(BF16))
    for i in range(n_items):
        ic, h = items[i]
        rows, cols = rows_of[i], cols_of[i]
        lhs = jnp.concatenate([kt_s[rows, cols].T.astype(BF16), attns[i]], axis=0)
        prod = jnp.dot(lhs, wus[i], preferred_element_type=F32)
        g16[ic, h] = (-prod[0:hd, 0:hd]).astype(BF16)
        h_s[ic, h] = prod[0:hd, hd:2 * hd]
        qe16[ic, h] = (qd_s[rows, cols] - prod[hd:hd + c, 0:hd]).astype(BF16)
        ou_s[ic, h] = prod[hd:hd + c, hd:2 * hd]

    def phase2_body(ic, carry):
        for h in range(n_heads):
            state = state_ref[h]
            s_b = state.astype(BF16)
            sc16[ic, h] = s_b
            state_ref[h] = (state * gl_s[h, pl.ds(ic, 1), :]
                            + jnp.dot(g16[ic, h], s_b, preferred_element_type=F32)
                            + h_s[ic, h])
        return carry

    lax.fori_loop(0, n_chunks, phase2_body, 0)

    nw = nw_ref[...]
    for i in range(n_items):
        ic, h = items[i]
        rows = rows_of[i]
        o = jnp.dot(qe16[ic, h], sc16[ic, h], preferred_element_type=F32) + ou_s[ic, h]
        ocols = slice(out_col + h * hd, out_col + (h + 1) * hd)
        out_ref[rows, ocols] = (o * _rms_scale(o) * nw * sz_ref[rows, cols_of[i]]).astype(out_ref.dtype)


def _mixer_kernel(x_ref, mod_ref, nw_ref, w_ref, gp_ref,
                  lcw_ref, lcb_ref, lgw_ref, lgb_ref, lam_ref, lnw_ref, gcw_ref, gnw_ref,
                  o_ref,
                  lru_cbuf, hprev_ref, sz_ref, cbuf_q, cbuf_k, cbuf_v, kb16, q16, k16, rhs16,
                  qd_s, kt_s, gcb_s, g16, h_s, qe16, ou_s, sc16, gl_s, state_ref,
                  *, n_heads, lru_width):
    gdn_width = n_heads * GDN_HEAD_DIM
    q_col = 2 * lru_width
    z_col = q_col + 3 * gdn_width
    ba_col = z_col + gdn_width

    @pl.when(pl.program_id(1) == 0)
    def _():
        for hist in (lru_cbuf, cbuf_q, cbuf_k, cbuf_v):
            hist[...] = jnp.zeros_like(hist)
        hprev_ref[...] = jnp.zeros_like(hprev_ref)
        state_ref[...] = jnp.zeros_like(state_ref)

    x = x_ref[...]
    h = ((x * _rms_scale(x) * nw_ref[...]) * (1.0 + mod_ref[1]) + mod_ref[0]).astype(BF16)

    def proj(lo, hi):
        return jnp.dot(h, w_ref[:, lo:hi], preferred_element_type=F32)

    x_lru = proj(0, lru_width)
    y_lru = proj(lru_width, q_col)
    q_pre = proj(q_col, q_col + gdn_width)
    k_pre = proj(q_col + gdn_width, q_col + 2 * gdn_width)
    v_pre = proj(q_col + 2 * gdn_width, z_col)
    sz_ref[...] = _silu(proj(z_col, ba_col))
    tail = proj(ba_col, ba_col + LANES)
    lane = lax.broadcasted_iota(jnp.int32, tail.shape, 1)
    g = gp_ref[0:1, :] * _softplus(tail + gp_ref[1:2, :])
    ba = jnp.where(lane < n_heads, _sigmoid(tail), jnp.where(lane < 2 * n_heads, g, 0.0))

    o_ref[:, 0:lru_width] = _lru_branch(
        x_lru, y_lru, lcw_ref, lcb_ref, lgw_ref, lgb_ref, lam_ref, lnw_ref,
        lru_cbuf, hprev_ref).astype(o_ref.dtype)
    _gdn_branch(q_pre, k_pre, v_pre, ba, gcw_ref, gnw_ref, sz_ref, o_ref, lru_width,
                cbuf_q, cbuf_k, cbuf_v, kb16, q16, k16, rhs16, qd_s, kt_s, gcb_s,
                g16, h_s, qe16, ou_s, sc16, gl_s, state_ref, n_heads=n_heads)


def _mixer(x, mod_l, norm_w, w_in_pad, gate_params, lru_conv_w, lru_conv_b, lru_gate_w,
           lru_gate_b, lru_lam, lru_norm_w, gdn_conv_w, gdn_norm_w, *, layer, ts, n_heads):
    bsz, seq, d = x.shape
    lru_width = lru_conv_w.shape[1]
    hd = GDN_HEAD_DIM
    c = GDN_CHUNK
    gdn_width = n_heads * hd
    n_chunks = ts // c
    n_pad = w_in_pad.shape[2]
    out_width = lru_width + gdn_width
    res = functools.partial(_resident, n_grid_axes=2)
    return pl.pallas_call(
        functools.partial(_mixer_kernel, n_heads=n_heads, lru_width=lru_width),
        grid=(bsz, seq // ts),
        in_specs=[
            pl.BlockSpec((None, ts, d), lambda b, t: (b, t, 0)),
            pl.BlockSpec((None, N_MOD, 1, d), lambda b, t: (b, 0, 0, 0)),
            res((1, d)),
            res((d, n_pad), layer=layer),
            res((SUBLANES, LANES)),
            res((CONV_WIDTH, lru_width)),
            res((1, lru_width)),
            res((lru_width // MXU_DIM, MXU_DIM, 2 * MXU_DIM)),
            res((2, lru_width)),
            res((1, lru_width)),
            res((1, lru_width)),
            res((CONV_WIDTH, 3 * gdn_width)),
            res((1, hd)),
        ],
        out_specs=pl.BlockSpec((None, ts, out_width), lambda b, t: (b, t, 0)),
        out_shape=jax.ShapeDtypeStruct((bsz, seq, out_width), BF16),
        scratch_shapes=[
            pltpu.VMEM((2, SUBLANES, lru_width), F32),
            pltpu.VMEM((1, lru_width), F32),
            pltpu.VMEM((ts, gdn_width), F32),
            pltpu.VMEM((2, SUBLANES, gdn_width), F32),
            pltpu.VMEM((2, SUBLANES, gdn_width), F32),
            pltpu.VMEM((2, SUBLANES, gdn_width), F32),
            pltpu.VMEM((ts, gdn_width), BF16),
            pltpu.VMEM((ts, gdn_width), BF16),
            pltpu.VMEM((ts, gdn_width), BF16),
            pltpu.VMEM((ts, 2 * gdn_width), BF16),
            pltpu.VMEM((ts, gdn_width), F32),
            pltpu.VMEM((ts, gdn_width), F32),
            pltpu.VMEM((ts, gdn_width), F32),
            pltpu.VMEM((n_chunks, n_heads, hd, hd), BF16),
            pltpu.VMEM((n_chunks, n_heads, hd, hd), F32),
            pltpu.VMEM((n_chunks, n_heads, c, hd), BF16),
            pltpu.VMEM((n_chunks, n_heads, c, hd), F32),
            pltpu.VMEM((n_chunks, n_heads, hd, hd), BF16),
            pltpu.VMEM((n_heads, n_chunks, hd), F32),
            pltpu.VMEM((n_heads, hd, hd), F32),
        ],
        compiler_params=_compiler_params(("arbitrary", "arbitrary")),
        name="mixer",
    )(x, mod_l, norm_w, w_in_pad, gate_params, lru_conv_w, lru_conv_b, lru_gate_w,
      lru_gate_b, lru_lam, lru_norm_w, gdn_conv_w, gdn_norm_w)


def _outmlp_kernel(x_ref, mix_ref, mod_ref, wo_ref, nw_ref, wu_ref, wd_ref,
                   fnw_ref, o_ref, *, final_norm, tf):
    mix = jnp.dot(mix_ref[...], wo_ref[...], preferred_element_type=F32)
    x1 = x_ref[...] + mod_ref[2] * mix
    h2 = ((x1 * _rms_scale(x1) * nw_ref[...]) * (1.0 + mod_ref[4]) + mod_ref[3]).astype(BF16)
    dff = wu_ref.shape[1]
    acc = None
    for j in range(dff // tf):
        up = jnp.dot(h2, wu_ref[:, j * tf:(j + 1) * tf], preferred_element_type=F32)
        act = jnp.square(jnp.maximum(up, 0.0)).astype(BF16)
        down = jnp.dot(act, wd_ref[j * tf:(j + 1) * tf, :], preferred_element_type=F32)
        acc = down if acc is None else acc + down
    y = x1 + mod_ref[5] * acc
    if final_norm:
        y = y * _rms_scale(y) * fnw_ref[...]
    o_ref[...] = y


def _outmlp(x2d, mixed2d, mod_l, w_out, norm_w, w_up, w_down, final_w,
            *, layer, seq, tm, tf, final_norm):
    t, d = x2d.shape
    dff = w_up.shape[2]
    tiles_per_seq = seq // tm
    res = functools.partial(_resident, n_grid_axes=1)
    return pl.pallas_call(
        functools.partial(_outmlp_kernel, final_norm=final_norm, tf=tf),
        grid=(t // tm,),
        in_specs=[
            pl.BlockSpec((tm, d), lambda i: (i, 0)),
            pl.BlockSpec((tm, mixed2d.shape[1]), lambda i: (i, 0)),
            pl.BlockSpec((None, N_MOD, 1, d), lambda i: (i // tiles_per_seq, 0, 0, 0)),
            res((d, d), layer=layer),
            res((1, d)),
            res((d, dff), layer=layer),
            res((dff, d), layer=layer),
            res((1, d)),
        ],
        out_specs=pl.BlockSpec((tm, d), lambda i: (i, 0)),
        out_shape=jax.ShapeDtypeStruct((t, d), F32),
        compiler_params=_compiler_params(("arbitrary",)),
        name="outproj_mlp",
    )(x2d, mixed2d, mod_l, w_out, norm_w, w_up, w_down, final_w)


def _block_diag_gates(gate_a, gate_x):
    g, n, _ = gate_a.shape
    per = MXU_DIM // n
    n_blk = g // per

    def diag(w):
        w = w.reshape(n_blk, per, n, n)
        eye = jnp.eye(per, dtype=w.dtype)
        return jnp.einsum("bpij,pq->bpiqj", w, eye).reshape(n_blk, MXU_DIM, MXU_DIM)

    return jnp.concatenate([diag(gate_a), diag(gate_x)], axis=2)


def kernel(x, c, norm_mix_w, norm_mlp_w, w_mod, b_mod, w_in, lru_conv_w, lru_conv_b,
           lru_gate_a_w, lru_gate_a_b, lru_gate_x_w, lru_gate_x_b, lru_lambda, lru_norm_w,
           gdn_conv_w, gdn_a_log, gdn_dt_bias, gdn_norm_w, w_out, w_up, w_down, final_norm_w):
    bsz, seq, d = x.shape
    depth = w_in.shape[0]
    lru_width = lru_conv_w.shape[2]
    n_heads = gdn_a_log.shape[1]
    gdn_width = n_heads * GDN_HEAD_DIM
    in_cols = w_in.shape[2]
    ba_col = 2 * lru_width + 4 * gdn_width
    assert in_cols == ba_col + 2 * n_heads
    n_pad = ba_col + LANES

    mod = _modulation(c, w_mod, b_mod)
    w_in_pad = jnp.pad(w_in.astype(BF16), ((0, 0), (0, 0), (0, n_pad - in_cols)))
    w_out_b = w_out.astype(BF16)
    w_up_b = w_up.astype(BF16)
    w_down_b = w_down.astype(BF16)
    gate_params = jnp.zeros((depth, SUBLANES, LANES), F32)
    gate_params = gate_params.at[:, 0, n_heads:2 * n_heads].set(-jnp.exp(gdn_a_log))
    gate_params = gate_params.at[:, 1, n_heads:2 * n_heads].set(gdn_dt_bias)

    for l in range(depth):
        gate_w = _block_diag_gates(lru_gate_a_w[l], lru_gate_x_w[l]).astype(BF16)
        gate_b = jnp.stack([lru_gate_a_b[l], lru_gate_x_b[l]])
        mixed = _mixer(x, mod[l], norm_mix_w[l][None], w_in_pad, gate_params[l],
                       lru_conv_w[l], lru_conv_b[l][None], gate_w, gate_b,
                       lru_lambda[l][None], lru_norm_w[l][None],
                       gdn_conv_w[l], gdn_norm_w[l][None], layer=l, ts=MIXER_ROWS,
                       n_heads=n_heads)
        x2d = _outmlp(x.reshape(bsz * seq, d), mixed.reshape(bsz * seq, lru_width + gdn_width),
                      mod[l], w_out_b, norm_mlp_w[l][None], w_up_b, w_down_b,
                      final_norm_w[None], layer=l, seq=seq, tm=MLP_ROWS, tf=MLP_FF_CHUNK,
                      final_norm=(l == depth - 1))
        x = x2d.reshape(bsz, seq, d)
    return x
```

```python
import functools
import math

import jax
import jax.numpy as jnp
from jax import lax
from jax.experimental import pallas as pl
from jax.experimental.pallas import tpu as pltpu

LRU_C = 8.0
GDN_HEAD_DIM = 128
GDN_CHUNK = 64
CONV_WIDTH = 4
N_MOD = 6
NORM_EPS = 1e-6

LANES = 128
SUBLANES = 8
MXU_DIM = 256
VMEM_LIMIT_BYTES = 56 * 1024 * 1024

MIXER_ROWS = 512
MLP_ROWS = 512
MLP_FF_CHUNK = 1024

F32 = jnp.float32
BF16 = jnp.bfloat16


def _compiler_params(semantics):
    return pltpu.CompilerParams(dimension_semantics=semantics,
                                vmem_limit_bytes=VMEM_LIMIT_BYTES)


def _resident(shape, n_grid_axes, layer=None):
    index = (0,) * len(shape)
    if layer is not None:
        shape = (None,) + tuple(shape)
        index = (layer,) + index
    if n_grid_axes == 1:
        index_map = lambda i: index
    else:
        index_map = lambda i, j: index
    return pl.BlockSpec(shape, index_map, pipeline_mode=pl.Buffered(1))


def _sigmoid(x):
    return 0.5 * jnp.tanh(0.5 * x) + 0.5


def _silu(x):
    half = 0.5 * x
    return half * jnp.tanh(half) + half


def _softplus(x):
    return jnp.maximum(x, 0.0) + jnp.log1p(jnp.exp(-jnp.abs(x)))


def _rms_scale(x):
    return lax.rsqrt(jnp.mean(x * x, axis=-1, keepdims=True) + NORM_EPS)


def _l2norm(t):
    return t * lax.rsqrt(jnp.sum(t * t, axis=-1, keepdims=True) + 1e-6)


def _mod_kernel(c_ref, w_ref, b_ref, o_ref):
    c_act = _silu(c_ref[...]).astype(BF16)
    o_ref[...] = jnp.dot(c_act, w_ref[...].astype(BF16),
                         preferred_element_type=F32) + b_ref[...]


def _modulation(c, w_mod, b_mod):
    depth, d, n = w_mod.shape
    b = c.shape[0]
    bp = max(SUBLANES, b)
    c_pad = jnp.zeros((bp, d), F32).at[:b].set(c)
    tn = n // N_MOD
    out = pl.pallas_call(
        _mod_kernel,
        grid=(depth, n // tn),
        in_specs=[
            pl.BlockSpec((bp, d), lambda l, j: (0, 0)),
            pl.BlockSpec((None, d, tn), lambda l, j: (l, 0, j)),
            pl.BlockSpec((None, 1, tn), lambda l, j: (l, 0, j)),
        ],
        out_specs=pl.BlockSpec((None, bp, tn), lambda l, j: (l, 0, j)),
        out_shape=jax.ShapeDtypeStruct((depth, bp, n), F32),
        compiler_params=_compiler_params(("arbitrary", "arbitrary")),
        name="modulation",
    )(c_pad, w_mod, b_mod.reshape(depth, 1, n))
    return out[:, :b].reshape(depth, b, N_MOD, 1, d)


def _causal_conv(hist_ref, x, w_ref):
    assert w_ref.shape[0] == CONV_WIDTH == 4
    ts = x.shape[0]

    def shifted(hist, val, k):
        cat = jnp.concatenate([hist, val], axis=0)
        return pltpu.roll(cat, k, axis=0)[SUBLANES:]

    x_prev = shifted(hist_ref[0], x, 1)
    v = x * w_ref[1:2, :] + x_prev * w_ref[0:1, :]
    y = x * w_ref[3:4, :] + x_prev * w_ref[2:3, :] + shifted(hist_ref[1], v, 2)
    hist_ref[0] = x[ts - SUBLANES:]
    hist_ref[1] = v[ts - SUBLANES:]
    return y


def _lru_branch(x_lru, y_lru, cw_ref, cb_ref, gw_ref, gb_ref, lam_ref, nw_ref,
                cbuf_ref, hprev_ref):
    ts, width = x_lru.shape
    xr = _causal_conv(cbuf_ref, x_lru, cw_ref) + cb_ref[...]

    xr_b = xr.astype(BF16)
    r_parts, i_parts = [], []
    for blk in range(width // MXU_DIM):
        cols = slice(blk * MXU_DIM, (blk + 1) * MXU_DIM)
        gates = jnp.dot(xr_b[:, cols], gw_ref[blk], preferred_element_type=F32)
        r_parts.append(gates[:, :MXU_DIM])
        i_parts.append(gates[:, MXU_DIM:])
    r = _sigmoid(jnp.concatenate(r_parts, axis=1) + gb_ref[0:1, :])
    i = _sigmoid(jnp.concatenate(i_parts, axis=1) + gb_ref[1:2, :])
    lam = lam_ref[...]
    log_sig_lam = jnp.minimum(lam, 0.0) - jnp.log1p(jnp.exp(-jnp.abs(lam)))
    log_a = LRU_C * r * log_sig_lam
    a = jnp.exp(log_a)
    mult = jnp.sqrt(jnp.maximum(1.0 - a * a, 1e-12))
    b = mult * (i * xr)

    sub = lax.broadcasted_iota(jnp.int32, (ts, width), 0) & (SUBLANES - 1)
    shift = 1
    while shift < SUBLANES:
        valid = sub >= shift
        a_sh = pltpu.roll(a, shift, axis=0)
        b_sh = pltpu.roll(b, shift, axis=0)
        b = b + jnp.where(valid, a * b_sh, 0.0)
        a = jnp.where(valid, a * a_sh, a)
        shift *= 2
    carry = hprev_ref[...]
    groups = []
    for r in range(ts // SUBLANES):
        rows = slice(r * SUBLANES, (r + 1) * SUBLANES)
        h_r = a[rows] * carry + b[rows]
        groups.append(h_r)
        carry = h_r[SUBLANES - 1:SUBLANES, :]
    b = jnp.concatenate(groups, axis=0)
    hprev_ref[...] = carry

    y3 = y_lru * y_lru * y_lru
    gelu = 0.5 * y_lru * (1.0 + jnp.tanh(math.sqrt(2.0 / math.pi) * (y_lru + 0.044715 * y3)))
    m = b * gelu
    return m * _rms_scale(m) * nw_ref[...]


def _gdn_branch(q_pre, k_pre, v_pre, ba, cw_ref, nw_ref, sz_ref, out_ref, out_col,
                cbuf_q, cbuf_k, cbuf_v, kb16, q16, k16, rhs16, qd_s, kt_s, gcb_s,
                g16, h_s, qe16, ou_s, sc16, gl_s, state_ref, *, n_heads, between_phases):
    ts = q_pre.shape[0]
    hd = GDN_HEAD_DIM
    c = GDN_CHUNK
    n_chunks = ts // c
    width = n_heads * hd

    ri = lax.broadcasted_iota(jnp.int32, (c, c), 0)
    ci = lax.broadcasted_iota(jnp.int32, (c, c), 1)
    tril = (ri >= ci).astype(F32)
    causal = ri >= ci
    strict = ri > ci
    eye = (ri == ci).astype(F32)
    pair_mask = (ri == ci + 1) & ((ri & 1) == 1)
    merge_masks = []
    sz = 2
    while sz < c:
        sh = sz.bit_length() - 1
        rb = ri >> sh
        merge_masks.append((rb == (ci >> sh) + 1) & ((rb & 1) == 1))
        sz *= 2

    qc = _silu(_causal_conv(cbuf_q, q_pre, cw_ref.at[:, 0:width]))
    kc = _silu(_causal_conv(cbuf_k, k_pre, cw_ref.at[:, width:2 * width]))
    vc = _silu(_causal_conv(cbuf_v, v_pre, cw_ref.at[:, 2 * width:3 * width]))
    gc_all = jnp.concatenate(
        [jnp.dot(tril, ba[ic * c:(ic + 1) * c], precision=lax.Precision.HIGHEST,
                 preferred_element_type=F32) for ic in range(n_chunks)], axis=0)
    for h in range(n_heads):
        cols = slice(h * hd, (h + 1) * hd)
        beta = jnp.broadcast_to(ba[:, h:h + 1], (ts, hd))
        gcb = jnp.broadcast_to(gc_all[:, n_heads + h:n_heads + h + 1], (ts, hd))
        gcb3 = gcb.reshape(n_chunks, c, hd)
        g_last = gcb3[:, c - 1:c, :]
        eg = jnp.exp(gcb)
        kt_scale = jnp.exp(g_last - gcb3).reshape(ts, hd)
        qn = _l2norm(qc[:, cols]) * (hd ** -0.5)
        kn = _l2norm(kc[:, cols])
        kb = kn * beta
        kb16[:, cols] = kb.astype(BF16)
        q16[:, cols] = qn.astype(BF16)
        k16[:, cols] = kn.astype(BF16)
        rhs16[:, 2 * h * hd:(2 * h + 1) * hd] = (vc[:, cols] * beta).astype(BF16)
        rhs16[:, (2 * h + 1) * hd:(2 * h + 2) * hd] = (kb * eg).astype(BF16)
        qd_s[:, cols] = qn * eg
        kt_s[:, cols] = kn * kt_scale
        gcb_s[:, cols] = gcb
        gl_s[h] = jnp.exp(g_last).reshape(n_chunks, hd)

    items = [(ic, h) for ic in range(n_chunks) for h in range(n_heads)]
    rows_of = [slice(ic * c, (ic + 1) * c) for ic, _ in items]
    cols_of = [slice(h * hd, (h + 1) * hd) for _, h in items]
    n_items = len(items)
    lmats, attns, ps = [], [], []
    for i in range(n_items):
        rows, cols = rows_of[i], cols_of[i]
        gcb = gcb_s[rows, cols]
        diff = gcb[:, 0:c] - gcb.T[0:c, :]
        decay = jnp.where(causal, jnp.exp(jnp.where(causal, diff, 0.0)), 0.0)
        a_lhs = jnp.concatenate([kb16[rows, cols], q16[rows, cols]], axis=0)
        kkqk = lax.dot_general(a_lhs, k16[rows, cols], (((1,), (1,)), ((), ())),
                               preferred_element_type=F32)
        lmat = jnp.where(strict, kkqk[0:c] * decay, 0.0)
        lmats.append(lmat)
        attns.append(jnp.where(causal, kkqk[c:2 * c] * decay, 0.0).astype(BF16))
        ps.append(eye - jnp.where(pair_mask, lmat, 0.0))
    for merge_mask in merge_masks:
        p_bs = [p.astype(BF16) for p in ps]
        cps = [jnp.dot(jnp.where(merge_mask, lmats[i], 0.0).astype(BF16), p_bs[i],
                       preferred_element_type=F32) for i in range(n_items)]
        ps = [ps[i] - jnp.dot(p_bs[i], cps[i].astype(BF16), preferred_element_type=F32)
              for i in range(n_items)]
    wus = []
    for i in range(n_items):
        h = items[i][1]
        uw = jnp.dot(ps[i].astype(BF16), rhs16[rows_of[i], 2 * h * hd:(2 * h + 2) * hd],
                     preferred_element_type=F32)
        wus.append(jnp.concatenate([uw[:, hd:2 * hd], uw[:, 0:hd]], axis=1).astype(BF16))
    for i in range(n_items):
        ic, h = items[i]
        rows, cols = rows_of[i], cols_of[i]
        lhs = jnp.concatenate([kt_s[rows, cols].T.astype(BF16), attns[i]], axis=0)
        prod = jnp.dot(lhs, wus[i], preferred_element_type=F32)
        g16[ic, h] = (-prod[0:hd, 0:hd]).astype(BF16)
        h_s[ic, h] = prod[0:hd, hd:2 * hd]
        qe16[ic, h] = (qd_s[rows, cols] - prod[hd:hd + c, 0:hd]).astype(BF16)
        ou_s[ic, h] = prod[hd:hd + c, hd:2 * hd]

    between_phases()

    states = [state_ref[h] for h in range(n_heads)]
    for ic in range(n_chunks):
        for h in range(n_heads):
            s_b = states[h].astype(BF16)
            sc16[ic, h] = s_b
            states[h] = (states[h] * gl_s[h, ic:ic + 1, :]
                         + jnp.dot(g16[ic, h], s_b, preferred_element_type=F32)
                         + h_s[ic, h])
    for h in range(n_heads):
        state_ref[h] = states[h]

    nw = nw_ref[...]
    for i in range(n_items):
        ic, h = items[i]
        rows = rows_of[i]
        o = jnp.dot(qe16[ic, h], sc16[ic, h], preferred_element_type=F32) + ou_s[ic, h]
        ocols = slice(out_col + h * hd, out_col + (h + 1) * hd)
        out_ref[rows, ocols] = (o * _rms_scale(o) * nw * sz_ref[rows, cols_of[i]]).astype(out_ref.dtype)


def _mixer_kernel(x_ref, mod_ref, nw_ref, w_ref, gp_ref,
                  lcw_ref, lcb_ref, lgw_ref, lgb_ref, lam_ref, lnw_ref, gcw_ref, gnw_ref,
                  o_ref,
                  lru_cbuf, hprev_ref, sz_ref, cbuf_q, cbuf_k, cbuf_v, kb16, q16, k16, rhs16,
                  qd_s, kt_s, gcb_s, g16, h_s, qe16, ou_s, sc16, gl_s, state_ref,
                  *, n_heads, lru_width):
    gdn_width = n_heads * GDN_HEAD_DIM
    q_col = 2 * lru_width
    z_col = q_col + 3 * gdn_width
    ba_col = z_col + gdn_width

    @pl.when(pl.program_id(1) == 0)
    def _():
        for hist in (lru_cbuf, cbuf_q, cbuf_k, cbuf_v):
            hist[...] = jnp.zeros_like(hist)
        hprev_ref[...] = jnp.zeros_like(hprev_ref)
        state_ref[...] = jnp.zeros_like(state_ref)

    x = x_ref[...]
    h = ((x * _rms_scale(x) * nw_ref[...]) * (1.0 + mod_ref[1]) + mod_ref[0]).astype(BF16)

    def proj(lo, hi):
        return jnp.dot(h, w_ref[:, lo:hi], preferred_element_type=F32)

    x_lru = proj(0, lru_width)
    y_lru = proj(lru_width, q_col)
    q_pre = proj(q_col, q_col + gdn_width)
    k_pre = proj(q_col + gdn_width, q_col + 2 * gdn_width)
    v_pre = proj(q_col + 2 * gdn_width, z_col)
    sz_ref[...] = _silu(proj(z_col, ba_col))
    tail = proj(ba_col, ba_col + LANES)
    lane = lax.broadcasted_iota(jnp.int32, tail.shape, 1)
    g = gp_ref[0:1, :] * _softplus(tail + gp_ref[1:2, :])
    ba = jnp.where(lane < n_heads, _sigmoid(tail), jnp.where(lane < 2 * n_heads, g, 0.0))

    def lru_branch():
        o_ref[:, 0:lru_width] = _lru_branch(
            x_lru, y_lru, lcw_ref, lcb_ref, lgw_ref, lgb_ref, lam_ref, lnw_ref,
            lru_cbuf, hprev_ref).astype(o_ref.dtype)

    _gdn_branch(q_pre, k_pre, v_pre, ba, gcw_ref, gnw_ref, sz_ref, o_ref, lru_width,
                cbuf_q, cbuf_k, cbuf_v, kb16, q16, k16, rhs16, qd_s, kt_s, gcb_s,
                g16, h_s, qe16, ou_s, sc16, gl_s, state_ref, n_heads=n_heads,
                between_phases=lru_branch)


def _mixer(x, mod_l, norm_w, w_in_pad, gate_params, lru_conv_w, lru_conv_b, lru_gate_w,
           lru_gate_b, lru_lam, lru_norm_w, gdn_conv_w, gdn_norm_w, *, layer, ts, n_heads):
    bsz, seq, d = x.shape
    lru_width = lru_conv_w.shape[1]
    hd = GDN_HEAD_DIM
    c = GDN_CHUNK
    gdn_width = n_heads * hd
    n_chunks = ts // c
    n_pad = w_in_pad.shape[2]
    out_width = lru_width + gdn_width
    res = functools.partial(_resident, n_grid_axes=2)
    return pl.pallas_call(
        functools.partial(_mixer_kernel, n_heads=n_heads, lru_width=lru_width),
        grid=(bsz, seq // ts),
        in_specs=[
            pl.BlockSpec((None, ts, d), lambda b, t: (b, t, 0)),
            pl.BlockSpec((None, N_MOD, 1, d), lambda b, t: (b, 0, 0, 0)),
            res((1, d)),
            res((d, n_pad), layer=layer),
            res((SUBLANES, LANES)),
            res((CONV_WIDTH, lru_width)),
            res((1, lru_width)),
            res((lru_width // MXU_DIM, MXU_DIM, 2 * MXU_DIM)),
            res((2, lru_width)),
            res((1, lru_width)),
            res((1, lru_width)),
            res((CONV_WIDTH, 3 * gdn_width)),
            res((1, hd)),
        ],
        out_specs=pl.BlockSpec((None, ts, out_width), lambda b, t: (b, t, 0)),
        out_shape=jax.ShapeDtypeStruct((bsz, seq, out_width), BF16),
        scratch_shapes=[
            pltpu.VMEM((2, SUBLANES, lru_width), F32),
            pltpu.VMEM((1, lru_width), F32),
            pltpu.VMEM((ts, gdn_width), F32),
            pltpu.VMEM((2, SUBLANES, gdn_width), F32),
            pltpu.VMEM((2, SUBLANES, gdn_width), F32),
            pltpu.VMEM((2, SUBLANES, gdn_width), F32),
            pltpu.VMEM((ts, gdn_width), BF16),
            pltpu.VMEM((ts, gdn_width), BF16),
            pltpu.VMEM((ts, gdn_width), BF16),
            pltpu.VMEM((ts, 2 * gdn_width), BF16),
            pltpu.VMEM((ts, gdn_width), F32),
            pltpu.VMEM((ts, gdn_width), F32),
            pltpu.VMEM((ts, gdn_width), F32),
            pltpu.VMEM((n_chunks, n_heads, hd, hd), BF16),
            pltpu.VMEM((n_chunks, n_heads, hd, hd), F32),
            pltpu.VMEM((n_chunks, n_heads, c, hd), BF16),
            pltpu.VMEM((n_chunks, n_heads, c, hd), F32),
            pltpu.VMEM((n_chunks, n_heads, hd, hd), BF16),
            pltpu.VMEM((n_heads, n_chunks, hd), F32),
            pltpu.VMEM((n_heads, hd, hd), F32),
        ],
        compiler_params=_compiler_params(("arbitrary", "arbitrary")),
        name="mixer",
    )(x, mod_l, norm_w, w_in_pad, gate_params, lru_conv_w, lru_conv_b, lru_gate_w,
      lru_gate_b, lru_lam, lru_norm_w, gdn_conv_w, gdn_norm_w)


def _outmlp_kernel(x_ref, mix_ref, mod_ref, wo_ref, nw_ref, wu_ref, wd_ref,
                   fnw_ref, o_ref, *, final_norm, tf):
    mix = jnp.dot(mix_ref[...], wo_ref[...], preferred_element_type=F32)
    x1 = x_ref[...] + mod_ref[2] * mix
    h2 = ((x1 * _rms_scale(x1) * nw_ref[...]) * (1.0 + mod_ref[4]) + mod_ref[3]).astype(BF16)
    dff = wu_ref.shape[1]
    acc = None
    for j in range(dff // tf):
        up = jnp.dot(h2, wu_ref[:, j * tf:(j + 1) * tf], preferred_element_type=F32)
        act = jnp.square(jnp.maximum(up, 0.0)).astype(BF16)
        down = jnp.dot(act, wd_ref[j * tf:(j + 1) * tf, :], preferred_element_type=F32)
        acc = down if acc is None else acc + down
    y = x1 + mod_ref[5] * acc
    if final_norm:
        y = y * _rms_scale(y) * fnw_ref[...]
    o_ref[...] = y


def _outmlp(x2d, mixed2d, mod_l, w_out, norm_w, w_up, w_down, final_w,
            *, layer, seq, tm, tf, final_norm):
    t, d = x2d.shape
    dff = w_up.shape[2]
    tiles_per_seq = seq // tm
    res = functools.partial(_resident, n_grid_axes=1)
    return pl.pallas_call(
        functools.partial(_outmlp_kernel, final_norm=final_norm, tf=tf),
        grid=(t // tm,),
        in_specs=[
            pl.BlockSpec((tm, d), lambda i: (i, 0)),
            pl.BlockSpec((tm, mixed2d.shape[1]), lambda i: (i, 0)),
            pl.BlockSpec((None, N_MOD, 1, d), lambda i: (i // tiles_per_seq, 0, 0, 0)),
            res((d, d), layer=layer),
            res((1, d)),
            res((d, dff), layer=layer),
            res((dff, d), layer=layer),
            res((1, d)),
        ],
        out_specs=pl.BlockSpec((tm, d), lambda i: (i, 0)),
        out_shape=jax.ShapeDtypeStruct((t, d), F32),
        compiler_params=_compiler_params(("arbitrary",)),
        name="outproj_mlp",
    )(x2d, mixed2d, mod_l, w_out, norm_w, w_up, w_down, final_w)


def _block_diag_gates(gate_a, gate_x):
    g, n, _ = gate_a.shape
    per = MXU_DIM // n
    n_blk = g // per

    def diag(w):
        w = w.reshape(n_blk, per, n, n)
        eye = jnp.eye(per, dtype=w.dtype)
        return jnp.einsum("bpij,pq->bpiqj", w, eye).reshape(n_blk, MXU_DIM, MXU_DIM)

    return jnp.concatenate([diag(gate_a), diag(gate_x)], axis=2)


def kernel(x, c, norm_mix_w, norm_mlp_w, w_mod, b_mod, w_in, lru_conv_w, lru_conv_b,
           lru_gate_a_w, lru_gate_a_b, lru_gate_x_w, lru_gate_x_b, lru_lambda, lru_norm_w,
           gdn_conv_w, gdn_a_log, gdn_dt_bias, gdn_norm_w, w_out, w_up, w_down, final_norm_w):
    bsz, seq, d = x.shape
    depth = w_in.shape[0]
    lru_width = lru_conv_w.shape[2]
    n_heads = gdn_a_log.shape[1]
    gdn_width = n_heads * GDN_HEAD_DIM
    in_cols = w_in.shape[2]
    ba_col = 2 * lru_width + 4 * gdn_width
    assert in_cols == ba_col + 2 * n_heads
    n_pad = ba_col + LANES

    mod = _modulation(c, w_mod, b_mod)
    w_in_pad = jnp.pad(w_in.astype(BF16), ((0, 0), (0, 0), (0, n_pad - in_cols)))
    w_out_b = w_out.astype(BF16)
    w_up_b = w_up.astype(BF16)
    w_down_b = w_down.astype(BF16)
    gate_params = jnp.zeros((depth, SUBLANES, LANES), F32)
    gate_params = gate_params.at[:, 0, n_heads:2 * n_heads].set(-jnp.exp(gdn_a_log))
    gate_params = gate_params.at[:, 1, n_heads:2 * n_heads].set(gdn_dt_bias)

    for l in range(depth):
        gate_w = _block_diag_gates(lru_gate_a_w[l], lru_gate_x_w[l]).astype(BF16)
        gate_b = jnp.stack([lru_gate_a_b[l], lru_gate_x_b[l]])
        mixed = _mixer(x, mod[l], norm_mix_w[l][None], w_in_pad, gate_params[l],
                       lru_conv_w[l], lru_conv_b[l][None], gate_w, gate_b,
                       lru_lambda[l][None], lru_norm_w[l][None],
                       gdn_conv_w[l], gdn_norm_w[l][None], layer=l, ts=MIXER_ROWS,
                       n_heads=n_heads)
        x2d = _outmlp(x.reshape(bsz * seq, d), mixed.reshape(bsz * seq, lru_width + gdn_width),
                      mod[l], w_out_b, norm_mlp_w[l][None], w_up_b, w_down_b,
                      final_norm_w[None], layer=l, seq=seq, tm=MLP_ROWS, tf=MLP_FF_CHUNK,
                      final_norm=(l == depth - 1))
        x = x2d.reshape(bsz, seq, d)
    return x
```

```python
import functools
import math

import jax
import jax.numpy as jnp
from jax import lax
from jax.experimental import pallas as pl
from jax.experimental.pallas import tpu as pltpu

LRU_C = 8.0
GDN_HEAD_DIM = 128
GDN_CHUNK = 64
CONV_WIDTH = 4
N_MOD = 6
NORM_EPS = 1e-6

LANES = 128
SUBLANES = 8
MXU_DIM = 256
VMEM_LIMIT_BYTES = 56 * 1024 * 1024

MIXER_ROWS = 512
MLP_ROWS = 512
MLP_FF_CHUNK = 1024

F32 = jnp.float32
BF16 = jnp.bfloat16


def _compiler_params(semantics):
    return pltpu.CompilerParams(dimension_semantics=semantics,
                                vmem_limit_bytes=VMEM_LIMIT_BYTES)


def _resident(shape, n_grid_axes, layer=None):
    index = (0,) * len(shape)
    if layer is not None:
        shape = (None,) + tuple(shape)
        index = (layer,) + index
    if n_grid_axes == 1:
        index_map = lambda i: index
    else:
        index_map = lambda i, j: index
    return pl.BlockSpec(shape, index_map, pipeline_mode=pl.Buffered(1))


def _sigmoid(x):
    return 0.5 * jnp.tanh(0.5 * x) + 0.5


def _silu(x):
    half = 0.5 * x
    return half * jnp.tanh(half) + half


def _softplus(x):
    return jnp.maximum(x, 0.0) + jnp.log1p(jnp.exp(-jnp.abs(x)))


def _rms_scale(x):
    return lax.rsqrt(jnp.mean(x * x, axis=-1, keepdims=True) + NORM_EPS)


def _l2norm(t):
    return t * lax.rsqrt(jnp.sum(t * t, axis=-1, keepdims=True) + 1e-6)


def _mod_kernel(c_ref, w_ref, b_ref, o_ref):
    c_act = _silu(c_ref[...]).astype(BF16)
    o_ref[...] = jnp.dot(c_act, w_ref[...].astype(BF16),
                         preferred_element_type=F32) + b_ref[...]


def _modulation(c, w_mod, b_mod):
    depth, d, n = w_mod.shape
    b = c.shape[0]
    bp = max(SUBLANES, b)
    c_pad = jnp.zeros((bp, d), F32).at[:b].set(c)
    tn = n // N_MOD
    out = pl.pallas_call(
        _mod_kernel,
        grid=(depth, n // tn),
        in_specs=[
            pl.BlockSpec((bp, d), lambda l, j: (0, 0)),
            pl.BlockSpec((None, d, tn), lambda l, j: (l, 0, j)),
            pl.BlockSpec((None, 1, tn), lambda l, j: (l, 0, j)),
        ],
        out_specs=pl.BlockSpec((None, bp, tn), lambda l, j: (l, 0, j)),
        out_shape=jax.ShapeDtypeStruct((depth, bp, n), F32),
        compiler_params=_compiler_params(("arbitrary", "arbitrary")),
        name="modulation",
    )(c_pad, w_mod, b_mod.reshape(depth, 1, n))
    return out[:, :b].reshape(depth, b, N_MOD, 1, d)


def _causal_conv(hist_ref, x, w_ref):
    assert w_ref.shape[0] == CONV_WIDTH == 4
    ts = x.shape[0]

    def shifted(hist, val, k):
        cat = jnp.concatenate([hist, val], axis=0)
        return pltpu.roll(cat, k, axis=0)[SUBLANES:]

    x_prev = shifted(hist_ref[0], x, 1)
    v = x * w_ref[1:2, :] + x_prev * w_ref[0:1, :]
    y = x * w_ref[3:4, :] + x_prev * w_ref[2:3, :] + shifted(hist_ref[1], v, 2)
    hist_ref[0] = x[ts - SUBLANES:]
    hist_ref[1] = v[ts - SUBLANES:]
    return y


def _lru_branch(x_lru, y_lru, cw_ref, cb_ref, gw_ref, gb_ref, lam_ref, nw_ref,
                cbuf_ref, hprev_ref):
    ts, width = x_lru.shape
    xr = _causal_conv(cbuf_ref, x_lru, cw_ref) + cb_ref[...]

    xr_b = xr.astype(BF16)
    r_parts, i_parts = [], []
    for blk in range(width // MXU_DIM):
        cols = slice(blk * MXU_DIM, (blk + 1) * MXU_DIM)
        gates = jnp.dot(xr_b[:, cols], gw_ref[blk], preferred_element_type=F32)
        r_parts.append(gates[:, :MXU_DIM])
        i_parts.append(gates[:, MXU_DIM:])
    r = _sigmoid(jnp.concatenate(r_parts, axis=1) + gb_ref[0:1, :])
    i = _sigmoid(jnp.concatenate(i_parts, axis=1) + gb_ref[1:2, :])
    lam = lam_ref[...]
    log_sig_lam = jnp.minimum(lam, 0.0) - jnp.log1p(jnp.exp(-jnp.abs(lam)))
    log_a = LRU_C * r * log_sig_lam
    a = jnp.exp(log_a)
    mult = jnp.sqrt(jnp.maximum(1.0 - a * a, 1e-12))
    b = mult * (i * xr)

    sub = lax.broadcasted_iota(jnp.int32, (ts, width), 0) & (SUBLANES - 1)
    shift = 1
    while shift < SUBLANES:
        valid = sub >= shift
        a_sh = pltpu.roll(a, shift, axis=0)
        b_sh = pltpu.roll(b, shift, axis=0)
        b = b + jnp.where(valid, a * b_sh, 0.0)
        a = jnp.where(valid, a * a_sh, a)
        shift *= 2
    carry = hprev_ref[...]
    groups = []
    for r in range(ts // SUBLANES):
        rows = slice(r * SUBLANES, (r + 1) * SUBLANES)
        h_r = a[rows] * carry + b[rows]
        groups.append(h_r)
        carry = h_r[SUBLANES - 1:SUBLANES, :]
    b = jnp.concatenate(groups, axis=0)
    hprev_ref[...] = carry

    y3 = y_lru * y_lru * y_lru
    gelu = 0.5 * y_lru * (1.0 + jnp.tanh(math.sqrt(2.0 / math.pi) * (y_lru + 0.044715 * y3)))
    m = b * gelu
    return m * _rms_scale(m) * nw_ref[...]


def _gdn_branch(q_pre, k_pre, v_pre, ba, cw_ref, nw_ref, sz_ref, out_ref, out_col,
                cbuf_q, cbuf_k, cbuf_v, kb16, q16, k16, rhs16, qd_s, kt_s, gcb_s,
                g16, h_s, qe16, ou_s, gl_s, state_ref, *, n_heads, between_phases):
    ts = q_pre.shape[0]
    hd = GDN_HEAD_DIM
    c = GDN_CHUNK
    n_chunks = ts // c
    width = n_heads * hd

    ri = lax.broadcasted_iota(jnp.int32, (c, c), 0)
    ci = lax.broadcasted_iota(jnp.int32, (c, c), 1)
    tril = (ri >= ci).astype(F32)
    causal = ri >= ci
    strict = ri > ci
    eye = (ri == ci).astype(F32)
    pair_mask = (ri == ci + 1) & ((ri & 1) == 1)
    merge_masks = []
    sz = 2
    while sz < c:
        sh = sz.bit_length() - 1
        rb = ri >> sh
        merge_masks.append((rb == (ci >> sh) + 1) & ((rb & 1) == 1))
        sz *= 2

    qc = _silu(_causal_conv(cbuf_q, q_pre, cw_ref.at[:, 0:width]))
    kc = _silu(_causal_conv(cbuf_k, k_pre, cw_ref.at[:, width:2 * width]))
    vc = _silu(_causal_conv(cbuf_v, v_pre, cw_ref.at[:, 2 * width:3 * width]))
    gc_all = jnp.concatenate(
        [jnp.dot(tril, ba[ic * c:(ic + 1) * c], precision=lax.Precision.HIGHEST,
                 preferred_element_type=F32) for ic in range(n_chunks)], axis=0)
    for h in range(n_heads):
        cols = slice(h * hd, (h + 1) * hd)
        beta = jnp.broadcast_to(ba[:, h:h + 1], (ts, hd))
        gcb = jnp.broadcast_to(gc_all[:, n_heads + h:n_heads + h + 1], (ts, hd))
        gcb3 = gcb.reshape(n_chunks, c, hd)
        g_last = gcb3[:, c - 1:c, :]
        eg = jnp.exp(gcb)
        kt_scale = jnp.exp(g_last - gcb3).reshape(ts, hd)
        qn = _l2norm(qc[:, cols]) * (hd ** -0.5)
        kn = _l2norm(kc[:, cols])
        kb = kn * beta
        kb16[:, cols] = kb.astype(BF16)
        q16[:, cols] = qn.astype(BF16)
        k16[:, cols] = kn.astype(BF16)
        rhs16[:, 2 * h * hd:(2 * h + 1) * hd] = (vc[:, cols] * beta).astype(BF16)
        rhs16[:, (2 * h + 1) * hd:(2 * h + 2) * hd] = (kb * eg).astype(BF16)
        qd_s[:, cols] = qn * eg
        kt_s[:, cols] = kn * kt_scale
        gcb_s[:, cols] = gcb
        gl_s[h] = jnp.exp(g_last).reshape(n_chunks, hd)

    items = [(ic, h) for ic in range(n_chunks) for h in range(n_heads)]
    rows_of = [slice(ic * c, (ic + 1) * c) for ic, _ in items]
    cols_of = [slice(h * hd, (h + 1) * hd) for _, h in items]
    n_items = len(items)
    lmats, attns, ps = [], [], []
    for i in range(n_items):
        rows, cols = rows_of[i], cols_of[i]
        gcb = gcb_s[rows, cols]
        diff = gcb[:, 0:c] - gcb.T[0:c, :]
        decay = jnp.where(causal, jnp.exp(jnp.where(causal, diff, 0.0)), 0.0)
        a_lhs = jnp.concatenate([kb16[rows, cols], q16[rows, cols]], axis=0)
        kkqk = lax.dot_general(a_lhs, k16[rows, cols], (((1,), (1,)), ((), ())),
                               preferred_element_type=F32)
        lmat = jnp.where(strict, kkqk[0:c] * decay, 0.0)
        lmats.append(lmat)
        attns.append(jnp.where(causal, kkqk[c:2 * c] * decay, 0.0).astype(BF16))
        ps.append(eye - jnp.where(pair_mask, lmat, 0.0))
    for merge_mask in merge_masks:
        p_bs = [p.astype(BF16) for p in ps]
        cps = [jnp.dot(jnp.where(merge_mask, lmats[i], 0.0).astype(BF16), p_bs[i],
                       preferred_element_type=F32) for i in range(n_items)]
        ps = [ps[i] - jnp.dot(p_bs[i], cps[i].astype(BF16), preferred_element_type=F32)
              for i in range(n_items)]
    wus = []
    for i in range(n_items):
        h = items[i][1]
        uw = jnp.dot(ps[i].astype(BF16), rhs16[rows_of[i], 2 * h * hd:(2 * h + 2) * hd],
                     preferred_element_type=F32)
        wus.append(jnp.concatenate([uw[:, hd:2 * hd], uw[:, 0:hd]], axis=1).astype(BF16))
    for i in range(n_items):
        ic, h = items[i]
        rows, cols = rows_of[i], cols_of[i]
        lhs = jnp.concatenate([kt_s[rows, cols].T.astype(BF16), attns[i]], axis=0)
        prod = jnp.dot(lhs, wus[i], preferred_element_type=F32)
        g16[ic, h] = (-prod[0:hd, 0:hd]).astype(BF16)
        h_s[ic, h] = prod[0:hd, hd:2 * hd]
        qe16[ic, h] = (qd_s[rows, cols] - prod[hd:hd + c, 0:hd]).astype(BF16)
        ou_s[ic, h] = prod[hd:hd + c, hd:2 * hd]

    between_phases()

    nw = nw_ref[...]
    states = [state_ref[h] for h in range(n_heads)]
    for ic in range(n_chunks):
        entry_states = []
        for h in range(n_heads):
            s_b = states[h].astype(BF16)
            entry_states.append(s_b)
            states[h] = (states[h] * gl_s[h, ic:ic + 1, :]
                         + jnp.dot(g16[ic, h], s_b, preferred_element_type=F32)
                         + h_s[ic, h])
        rows = slice(ic * c, (ic + 1) * c)
        for h in range(n_heads):
            o = jnp.dot(qe16[ic, h], entry_states[h], preferred_element_type=F32) + ou_s[ic, h]
            ocols = slice(out_col + h * hd, out_col + (h + 1) * hd)
            out_ref[rows, ocols] = (o * _rms_scale(o) * nw
                                    * sz_ref[rows, h * hd:(h + 1) * hd]).astype(out_ref.dtype)
    for h in range(n_heads):
        state_ref[h] = states[h]


def _mixer_kernel(x_ref, mod_ref, nw_ref, w_ref, gp_ref,
                  lcw_ref, lcb_ref, lgw_ref, lgb_ref, lam_ref, lnw_ref, gcw_ref, gnw_ref,
                  o_ref,
                  lru_cbuf, hprev_ref, sz_ref, cbuf_q, cbuf_k, cbuf_v, kb16, q16, k16, rhs16,
                  qd_s, kt_s, gcb_s, g16, h_s, qe16, ou_s, gl_s, state_ref,
                  *, n_heads, lru_width):
    gdn_width = n_heads * GDN_HEAD_DIM
    q_col = 2 * lru_width
    z_col = q_col + 3 * gdn_width
    ba_col = z_col + gdn_width

    @pl.when(pl.program_id(1) == 0)
    def _():
        for hist in (lru_cbuf, cbuf_q, cbuf_k, cbuf_v):
            hist[...] = jnp.zeros_like(hist)
        hprev_ref[...] = jnp.zeros_like(hprev_ref)
        state_ref[...] = jnp.zeros_like(state_ref)

    x = x_ref[...]
    h = ((x * _rms_scale(x) * nw_ref[...]) * (1.0 + mod_ref[1]) + mod_ref[0]).astype(BF16)

    def proj(lo, hi):
        return jnp.dot(h, w_ref[:, lo:hi], preferred_element_type=F32)

    x_lru = proj(0, lru_width)
    y_lru = proj(lru_width, q_col)
    q_pre = proj(q_col, q_col + gdn_width)
    k_pre = proj(q_col + gdn_width, q_col + 2 * gdn_width)
    v_pre = proj(q_col + 2 * gdn_width, z_col)
    sz_ref[...] = _silu(proj(z_col, ba_col))
    tail = proj(ba_col, ba_col + LANES)
    lane = lax.broadcasted_iota(jnp.int32, tail.shape, 1)
    g = gp_ref[0:1, :] * _softplus(tail + gp_ref[1:2, :])
    ba = jnp.where(lane < n_heads, _sigmoid(tail), jnp.where(lane < 2 * n_heads, g, 0.0))

    def lru_branch():
        o_ref[:, 0:lru_width] = _lru_branch(
            x_lru, y_lru, lcw_ref, lcb_ref, lgw_ref, lgb_ref, lam_ref, lnw_ref,
            lru_cbuf, hprev_ref).astype(o_ref.dtype)

    _gdn_branch(q_pre, k_pre, v_pre, ba, gcw_ref, gnw_ref, sz_ref, o_ref, lru_width,
                cbuf_q, cbuf_k, cbuf_v, kb16, q16, k16, rhs16, qd_s, kt_s, gcb_s,
                g16, h_s, qe16, ou_s, gl_s, state_ref, n_heads=n_heads,
                between_phases=lru_branch)


def _mixer(x, mod_l, norm_w, w_in_pad, gate_params, lru_conv_w, lru_conv_b, lru_gate_w,
           lru_gate_b, lru_lam, lru_norm_w, gdn_conv_w, gdn_norm_w, *, layer, ts, n_heads):
    bsz, seq, d = x.shape
    lru_width = lru_conv_w.shape[1]
    hd = GDN_HEAD_DIM
    c = GDN_CHUNK
    gdn_width = n_heads * hd
    n_chunks = ts // c
    n_pad = w_in_pad.shape[2]
    out_width = lru_width + gdn_width
    res = functools.partial(_resident, n_grid_axes=2)
    return pl.pallas_call(
        functools.partial(_mixer_kernel, n_heads=n_heads, lru_width=lru_width),
        grid=(bsz, seq // ts),
        in_specs=[
            pl.BlockSpec((None, ts, d), lambda b, t: (b, t, 0)),
            pl.BlockSpec((None, N_MOD, 1, d), lambda b, t: (b, 0, 0, 0)),
            res((1, d)),
            res((d, n_pad), layer=layer),
            res((SUBLANES, LANES)),
            res((CONV_WIDTH, lru_width)),
            res((1, lru_width)),
            res((lru_width // MXU_DIM, MXU_DIM, 2 * MXU_DIM)),
            res((2, lru_width)),
            res((1, lru_width)),
            res((1, lru_width)),
            res((CONV_WIDTH, 3 * gdn_width)),
            res((1, hd)),
        ],
        out_specs=pl.BlockSpec((None, ts, out_width), lambda b, t: (b, t, 0)),
        out_shape=jax.ShapeDtypeStruct((bsz, seq, out_width), BF16),
        scratch_shapes=[
            pltpu.VMEM((2, SUBLANES, lru_width), F32),
            pltpu.VMEM((1, lru_width), F32),
            pltpu.VMEM((ts, gdn_width), F32),
            pltpu.VMEM((2, SUBLANES, gdn_width), F32),
            pltpu.VMEM((2, SUBLANES, gdn_width), F32),
            pltpu.VMEM((2, SUBLANES, gdn_width), F32),
            pltpu.VMEM((ts, gdn_width), BF16),
            pltpu.VMEM((ts, gdn_width), BF16),
            pltpu.VMEM((ts, gdn_width), BF16),
            pltpu.VMEM((ts, 2 * gdn_width), BF16),
            pltpu.VMEM((ts, gdn_width), F32),
            pltpu.VMEM((ts, gdn_width), F32),
            pltpu.VMEM((ts, gdn_width), F32),
            pltpu.VMEM((n_chunks, n_heads, hd, hd), BF16),
            pltpu.VMEM((n_chunks, n_heads, hd, hd), F32),
            pltpu.VMEM((n_chunks, n_heads, c, hd), BF16),
            pltpu.VMEM((n_chunks, n_heads, c, hd), F32),
            pltpu.VMEM((n_heads, n_chunks, hd), F32),
            pltpu.VMEM((n_heads, hd, hd), F32),
        ],
        compiler_params=_compiler_params(("arbitrary", "arbitrary")),
        name="mixer",
    )(x, mod_l, norm_w, w_in_pad, gate_params, lru_conv_w, lru_conv_b, lru_gate_w,
      lru_gate_b, lru_lam, lru_norm_w, gdn_conv_w, gdn_norm_w)


def _outmlp_kernel(x_ref, mix_ref, mod_ref, wo_ref, nw_ref, wu_ref, wd_ref,
                   fnw_ref, o_ref, *, final_norm, tf):
    mix = jnp.dot(mix_ref[...], wo_ref[...], preferred_element_type=F32)
    x1 = x_ref[...] + mod_ref[2] * mix
    h2 = ((x1 * _rms_scale(x1) * nw_ref[...]) * (1.0 + mod_ref[4]) + mod_ref[3]).astype(BF16)
    dff = wu_ref.shape[1]
    acc = None
    for j in range(dff // tf):
        up = jnp.dot(h2, wu_ref[:, j * tf:(j + 1) * tf], preferred_element_type=F32)
        act = jnp.square(jnp.maximum(up, 0.0)).astype(BF16)
        down = jnp.dot(act, wd_ref[j * tf:(j + 1) * tf, :], preferred_element_type=F32)
        acc = down if acc is None else acc + down
    y = x1 + mod_ref[5] * acc
    if final_norm:
        y = y * _rms_scale(y) * fnw_ref[...]
    o_ref[...] = y


def _outmlp(x2d, mixed2d, mod_l, w_out, norm_w, w_up, w_down, final_w,
            *, layer, seq, tm, tf, final_norm):
    t, d = x2d.shape
    dff = w_up.shape[2]
    tiles_per_seq = seq // tm
    res = functools.partial(_resident, n_grid_axes=1)
    return pl.pallas_call(
        functools.partial(_outmlp_kernel, final_norm=final_norm, tf=tf),
        grid=(t // tm,),
        in_specs=[
            pl.BlockSpec((tm, d), lambda i: (i, 0)),
            pl.BlockSpec((tm, mixed2d.shape[1]), lambda i: (i, 0)),
            pl.BlockSpec((None, N_MOD, 1, d), lambda i: (i // tiles_per_seq, 0, 0, 0)),
            res((d, d), layer=layer),
            res((1, d)),
            res((d, dff), layer=layer),
            res((dff, d), layer=layer),
            res((1, d)),
        ],
        out_specs=pl.BlockSpec((tm, d), lambda i: (i, 0)),
        out_shape=jax.ShapeDtypeStruct((t, d), F32),
        compiler_params=_compiler_params(("arbitrary",)),
        name="outproj_mlp",
    )(x2d, mixed2d, mod_l, w_out, norm_w, w_up, w_down, final_w)


def _block_diag_gates(gate_a, gate_x):
    g, n, _ = gate_a.shape
    per = MXU_DIM // n
    n_blk = g // per

    def diag(w):
        w = w.reshape(n_blk, per, n, n)
        eye = jnp.eye(per, dtype=w.dtype)
        return jnp.einsum("bpij,pq->bpiqj", w, eye).reshape(n_blk, MXU_DIM, MXU_DIM)

    return jnp.concatenate([diag(gate_a), diag(gate_x)], axis=2)


def kernel(x, c, norm_mix_w, norm_mlp_w, w_mod, b_mod, w_in, lru_conv_w, lru_conv_b,
           lru_gate_a_w, lru_gate_a_b, lru_gate_x_w, lru_gate_x_b, lru_lambda, lru_norm_w,
           gdn_conv_w, gdn_a_log, gdn_dt_bias, gdn_norm_w, w_out, w_up, w_down, final_norm_w):
    bsz, seq, d = x.shape
    depth = w_in.shape[0]
    lru_width = lru_conv_w.shape[2]
    n_heads = gdn_a_log.shape[1]
    gdn_width = n_heads * GDN_HEAD_DIM
    in_cols = w_in.shape[2]
    ba_col = 2 * lru_width + 4 * gdn_width
    assert in_cols == ba_col + 2 * n_heads
    n_pad = ba_col + LANES

    mod = _modulation(c, w_mod, b_mod)
    w_in_pad = jnp.pad(w_in.astype(BF16), ((0, 0), (0, 0), (0, n_pad - in_cols)))
    w_out_b = w_out.astype(BF16)
    w_up_b = w_up.astype(BF16)
    w_down_b = w_down.astype(BF16)
    gate_params = jnp.zeros((depth, SUBLANES, LANES), F32)
    gate_params = gate_params.at[:, 0, n_heads:2 * n_heads].set(-jnp.exp(gdn_a_log))
    gate_params = gate_params.at[:, 1, n_heads:2 * n_heads].set(gdn_dt_bias)

    for l in range(depth):
        gate_w = _block_diag_gates(lru_gate_a_w[l], lru_gate_x_w[l]).astype(BF16)
        gate_b = jnp.stack([lru_gate_a_b[l], lru_gate_x_b[l]])
        mixed = _mixer(x, mod[l], norm_mix_w[l][None], w_in_pad, gate_params[l],
                       lru_conv_w[l], lru_conv_b[l][None], gate_w, gate_b,
                       lru_lambda[l][None], lru_norm_w[l][None],
                       gdn_conv_w[l], gdn_norm_w[l][None], layer=l, ts=MIXER_ROWS,
                       n_heads=n_heads)
        x2d = _outmlp(x.reshape(bsz * seq, d), mixed.reshape(bsz * seq, lru_width + gdn_width),
                      mod[l], w_out_b, norm_mlp_w[l][None], w_up_b, w_down_b,
                      final_norm_w[None], layer=l, seq=seq, tm=MLP_ROWS, tf=MLP_FF_CHUNK,
                      final_norm=(l == depth - 1))
        x = x2d.reshape(bsz, seq, d)
    return x
```

```python
import functools
import math

import jax
import jax.numpy as jnp
from jax import lax
from jax.experimental import pallas as pl
from jax.experimental.pallas import tpu as pltpu

LRU_C = 8.0
GDN_HEAD_DIM = 128
GDN_CHUNK = 64
CONV_WIDTH = 4
N_MOD = 6
NORM_EPS = 1e-6

LANES = 128
SUBLANES = 8
MXU_DIM = 256
VMEM_LIMIT_BYTES = 56 * 1024 * 1024

MIXER_ROWS = 512
MLP_ROWS = 512
MLP_FF_CHUNK = 1024

F32 = jnp.float32
BF16 = jnp.bfloat16


def _compiler_params(semantics):
    return pltpu.CompilerParams(dimension_semantics=semantics,
                                vmem_limit_bytes=VMEM_LIMIT_BYTES)


def _resident(shape, n_grid_axes, layer=None):
    index = (0,) * len(shape)
    if layer is not None:
        shape = (None,) + tuple(shape)
        index = (layer,) + index
    if n_grid_axes == 1:
        index_map = lambda i: index
    else:
        index_map = lambda i, j: index
    return pl.BlockSpec(shape, index_map, pipeline_mode=pl.Buffered(1))


def _sigmoid(x):
    return 0.5 * jnp.tanh(0.5 * x) + 0.5


def _silu(x):
    half = 0.5 * x
    return half * jnp.tanh(half) + half


def _softplus(x):
    return jnp.maximum(x, 0.0) + jnp.log1p(jnp.exp(-jnp.abs(x)))


def _rms_scale(x):
    return lax.rsqrt(jnp.mean(x * x, axis=-1, keepdims=True) + NORM_EPS)


def _l2norm(t, scale=1.0):
    inv = lax.rsqrt(jnp.sum(t * t, axis=-1, keepdims=True) + 1e-6)
    return t * (inv if scale == 1.0 else inv * scale)


def _mod_kernel(c_ref, w_ref, b_ref, o_ref):
    c_act = _silu(c_ref[...]).astype(BF16)
    o_ref[...] = jnp.dot(c_act, w_ref[...].astype(BF16),
                         preferred_element_type=F32) + b_ref[...]


def _modulation(c, w_mod, b_mod):
    depth, d, n = w_mod.shape
    b = c.shape[0]
    bp = max(SUBLANES, b)
    c_pad = jnp.zeros((bp, d), F32).at[:b].set(c)
    tn = n // N_MOD
    out = pl.pallas_call(
        _mod_kernel,
        grid=(depth, n // tn),
        in_specs=[
            pl.BlockSpec((bp, d), lambda l, j: (0, 0)),
            pl.BlockSpec((None, d, tn), lambda l, j: (l, 0, j)),
            pl.BlockSpec((None, 1, tn), lambda l, j: (l, 0, j)),
        ],
        out_specs=pl.BlockSpec((None, bp, tn), lambda l, j: (l, 0, j)),
        out_shape=jax.ShapeDtypeStruct((depth, bp, n), F32),
        compiler_params=_compiler_params(("arbitrary", "arbitrary")),
        name="modulation",
    )(c_pad, w_mod, b_mod.reshape(depth, 1, n))
    return out[:, :b].reshape(depth, b, N_MOD, 1, d)


def _causal_conv(hist_ref, x, w_ref):
    assert w_ref.shape[0] == CONV_WIDTH == 4
    ts = x.shape[0]

    def shifted(hist, val, k):
        cat = jnp.concatenate([hist, val], axis=0)
        return pltpu.roll(cat, k, axis=0)[SUBLANES:]

    x_prev = shifted(hist_ref[0], x, 1)
    v = x * w_ref[1:2, :] + x_prev * w_ref[0:1, :]
    y = x * w_ref[3:4, :] + x_prev * w_ref[2:3, :] + shifted(hist_ref[1], v, 2)
    hist_ref[0] = x[ts - SUBLANES:]
    hist_ref[1] = v[ts - SUBLANES:]
    return y


def _lru_branch(x_lru, y_lru, cw_ref, cb_ref, gw_ref, gb_ref, lam_ref, nw_ref,
                cbuf_ref, hprev_ref):
    ts, width = x_lru.shape
    xr = _causal_conv(cbuf_ref, x_lru, cw_ref) + cb_ref[...]

    xr_b = xr.astype(BF16)
    r_parts, i_parts = [], []
    for blk in range(width // MXU_DIM):
        cols = slice(blk * MXU_DIM, (blk + 1) * MXU_DIM)
        gates = jnp.dot(xr_b[:, cols], gw_ref[blk], preferred_element_type=F32)
        r_parts.append(gates[:, :MXU_DIM])
        i_parts.append(gates[:, MXU_DIM:])
    r = _sigmoid(jnp.concatenate(r_parts, axis=1) + gb_ref[0:1, :])
    i = _sigmoid(jnp.concatenate(i_parts, axis=1) + gb_ref[1:2, :])
    lam = lam_ref[...]
    log_sig_lam = jnp.minimum(lam, 0.0) - jnp.log1p(jnp.exp(-jnp.abs(lam)))
    log_a = LRU_C * r * log_sig_lam
    a = jnp.exp(log_a)
    mult = jnp.sqrt(jnp.maximum(1.0 - a * a, 1e-12))
    b = mult * (i * xr)

    sub = lax.broadcasted_iota(jnp.int32, (ts, width), 0) & (SUBLANES - 1)
    shift = 1
    while shift < SUBLANES:
        valid = sub >= shift
        a_sh = pltpu.roll(a, shift, axis=0)
        b_sh = pltpu.roll(b, shift, axis=0)
        b = b + jnp.where(valid, a * b_sh, 0.0)
        a = jnp.where(valid, a * a_sh, a)
        shift *= 2
    carry = hprev_ref[...]
    groups = []
    for r in range(ts // SUBLANES):
        rows = slice(r * SUBLANES, (r + 1) * SUBLANES)
        h_r = a[rows] * carry + b[rows]
        groups.append(h_r)
        carry = h_r[SUBLANES - 1:SUBLANES, :]
    b = jnp.concatenate(groups, axis=0)
    hprev_ref[...] = carry

    y3 = y_lru * y_lru * y_lru
    gelu = 0.5 * y_lru * (1.0 + jnp.tanh(math.sqrt(2.0 / math.pi) * (y_lru + 0.044715 * y3)))
    m = b * gelu
    return m * _rms_scale(m) * nw_ref[...]


def _gdn_branch(q_pre, k_pre, v_pre, ba, cw_ref, nw_ref, sz_ref, out_ref, out_col,
                cbuf_q, cbuf_k, cbuf_v, kb16, q16, k16, rhs16, qd_s, kt_s, gcb_s,
                g16, h_s, qe16, ou_s, gl_s, state_ref, *, n_heads, between_phases):
    ts = q_pre.shape[0]
    hd = GDN_HEAD_DIM
    c = GDN_CHUNK
    n_chunks = ts // c
    width = n_heads * hd

    ri = lax.broadcasted_iota(jnp.int32, (c, c), 0)
    ci = lax.broadcasted_iota(jnp.int32, (c, c), 1)
    tril = (ri >= ci).astype(F32)
    causal = ri >= ci
    strict = ri > ci
    eye = (ri == ci).astype(F32)
    pair_mask = (ri == ci + 1) & ((ri & 1) == 1)
    merge_masks = []
    sz = 2
    while sz < c:
        sh = sz.bit_length() - 1
        rb = ri >> sh
        merge_masks.append((rb == (ci >> sh) + 1) & ((rb & 1) == 1))
        sz *= 2

    qc = _silu(_causal_conv(cbuf_q, q_pre, cw_ref.at[:, 0:width]))
    kc = _silu(_causal_conv(cbuf_k, k_pre, cw_ref.at[:, width:2 * width]))
    vc = _silu(_causal_conv(cbuf_v, v_pre, cw_ref.at[:, 2 * width:3 * width]))
    gc_all = jnp.concatenate(
        [jnp.dot(tril, ba[ic * c:(ic + 1) * c], precision=lax.Precision.HIGHEST,
                 preferred_element_type=F32) for ic in range(n_chunks)], axis=0)
    for h in range(n_heads):
        cols = slice(h * hd, (h + 1) * hd)
        beta = jnp.broadcast_to(ba[:, h:h + 1], (ts, hd))
        gcb = jnp.broadcast_to(gc_all[:, n_heads + h:n_heads + h + 1], (ts, hd))
        gcb3 = gcb.reshape(n_chunks, c, hd)
        g_last = gcb3[:, c - 1:c, :]
        eg = jnp.exp(gcb)
        kt_scale = jnp.exp(g_last - gcb3).reshape(ts, hd)
        qn = _l2norm(qc[:, cols], scale=hd ** -0.5)
        kn = _l2norm(kc[:, cols])
        kb = kn * beta
        kb16[:, cols] = kb.astype(BF16)
        q16[:, cols] = qn.astype(BF16)
        k16[:, cols] = kn.astype(BF16)
        rhs16[:, 2 * h * hd:(2 * h + 1) * hd] = (vc[:, cols] * beta).astype(BF16)
        rhs16[:, (2 * h + 1) * hd:(2 * h + 2) * hd] = (kb * eg).astype(BF16)
        qd_s[:, cols] = qn * eg
        kt_s[:, cols] = kn * kt_scale
        gcb_s[:, cols] = gcb
        gl_s[h] = jnp.exp(g_last).reshape(n_chunks, hd)

    items = [(ic, h) for ic in range(n_chunks) for h in range(n_heads)]
    rows_of = [slice(ic * c, (ic + 1) * c) for ic, _ in items]
    cols_of = [slice(h * hd, (h + 1) * hd) for _, h in items]
    n_items = len(items)
    lmats, attns, ps = [], [], []
    for i in range(n_items):
        rows, cols = rows_of[i], cols_of[i]
        gcb = gcb_s[rows, cols]
        diff = gcb[:, 0:c] - gcb.T[0:c, :]
        decay = jnp.exp(jnp.where(causal, diff, 0.0))
        a_lhs = jnp.concatenate([kb16[rows, cols], q16[rows, cols]], axis=0)
        kkqk = lax.dot_general(a_lhs, k16[rows, cols], (((1,), (1,)), ((), ())),
                               preferred_element_type=F32)
        lmat = jnp.where(strict, kkqk[0:c] * decay, 0.0)
        lmats.append(lmat)
        attns.append(jnp.where(causal, kkqk[c:2 * c] * decay, 0.0).astype(BF16))
        ps.append(eye - jnp.where(pair_mask, lmat, 0.0))
    l_bs = [lmat.astype(BF16) for lmat in lmats]
    for merge_mask in merge_masks:
        p_bs = [p.astype(BF16) for p in ps]
        cps = [jnp.where(merge_mask,
                         jnp.dot(l_bs[i], p_bs[i], preferred_element_type=F32), 0.0)
               for i in range(n_items)]
        ps = [ps[i] - jnp.dot(p_bs[i], cps[i].astype(BF16), preferred_element_type=F32)
              for i in range(n_items)]
    wus = []
    for i in range(n_items):
        h = items[i][1]
        uw = jnp.dot(ps[i].astype(BF16), rhs16[rows_of[i], 2 * h * hd:(2 * h + 2) * hd],
                     preferred_element_type=F32)
        wus.append(jnp.concatenate([uw[:, hd:2 * hd], uw[:, 0:hd]], axis=1).astype(BF16))
    for i in range(n_items):
        ic, h = items[i]
        rows, cols = rows_of[i], cols_of[i]
        lhs = jnp.concatenate([kt_s[rows, cols].T.astype(BF16), attns[i]], axis=0)
        prod = jnp.dot(lhs, wus[i], preferred_element_type=F32)
        g16[ic, h] = (-prod[0:hd, 0:hd]).astype(BF16)
        h_s[ic, h] = prod[0:hd, hd:2 * hd]
        qe16[ic, h] = (qd_s[rows, cols] - prod[hd:hd + c, 0:hd]).astype(BF16)
        ou_s[ic, h] = prod[hd:hd + c, hd:2 * hd]

    between_phases()

    nw = nw_ref[...]
    states = [state_ref[h] for h in range(n_heads)]
    for ic in range(n_chunks):
        entry_states = []
        for h in range(n_heads):
            s_b = states[h].astype(BF16)
            entry_states.append(s_b)
            states[h] = (states[h] * gl_s[h, ic:ic + 1, :]
                         + jnp.dot(g16[ic, h], s_b, preferred_element_type=F32)
                         + h_s[ic, h])
        rows = slice(ic * c, (ic + 1) * c)
        for h in range(n_heads):
            o = jnp.dot(qe16[ic, h], entry_states[h], preferred_element_type=F32) + ou_s[ic, h]
            ocols = slice(out_col + h * hd, out_col + (h + 1) * hd)
            out_ref[rows, ocols] = (o * _rms_scale(o) * nw
                                    * sz_ref[rows, h * hd:(h + 1) * hd]).astype(out_ref.dtype)
    for h in range(n_heads):
        state_ref[h] = states[h]


def _mixer_kernel(x_ref, mod_ref, nw_ref, w_ref, gp_ref,
                  lcw_ref, lcb_ref, lgw_ref, lgb_ref, lam_ref, lnw_ref, gcw_ref, gnw_ref,
                  o_ref,
                  lru_cbuf, hprev_ref, sz_ref, cbuf_q, cbuf_k, cbuf_v, kb16, q16, k16, rhs16,
                  qd_s, kt_s, gcb_s, g16, h_s, qe16, ou_s, gl_s, state_ref,
                  *, n_heads, lru_width):
    gdn_width = n_heads * GDN_HEAD_DIM
    q_col = 2 * lru_width
    z_col = q_col + 3 * gdn_width
    ba_col = z_col + gdn_width

    @pl.when(pl.program_id(1) == 0)
    def _():
        for hist in (lru_cbuf, cbuf_q, cbuf_k, cbuf_v):
            hist[...] = jnp.zeros_like(hist)
        hprev_ref[...] = jnp.zeros_like(hprev_ref)
        state_ref[...] = jnp.zeros_like(state_ref)

    x = x_ref[...]
    gain = nw_ref[...] * (1.0 + mod_ref[1])
    h = ((x * _rms_scale(x)) * gain + mod_ref[0]).astype(BF16)

    def proj(lo, hi):
        return jnp.dot(h, w_ref[:, lo:hi], preferred_element_type=F32)

    x_lru = proj(0, lru_width)
    y_lru = proj(lru_width, q_col)
    q_pre = proj(q_col, q_col + gdn_width)
    k_pre = proj(q_col + gdn_width, q_col + 2 * gdn_width)
    v_pre = proj(q_col + 2 * gdn_width, z_col)
    sz_ref[...] = _silu(proj(z_col, ba_col))
    tail = proj(ba_col, ba_col + LANES)
    lane = lax.broadcasted_iota(jnp.int32, tail.shape, 1)
    g = gp_ref[0:1, :] * _softplus(tail + gp_ref[1:2, :])
    ba = jnp.where(lane < n_heads, _sigmoid(tail), jnp.where(lane < 2 * n_heads, g, 0.0))

    def lru_branch():
        o_ref[:, 0:lru_width] = _lru_branch(
            x_lru, y_lru, lcw_ref, lcb_ref, lgw_ref, lgb_ref, lam_ref, lnw_ref,
            lru_cbuf, hprev_ref).astype(o_ref.dtype)

    _gdn_branch(q_pre, k_pre, v_pre, ba, gcw_ref, gnw_ref, sz_ref, o_ref, lru_width,
                cbuf_q, cbuf_k, cbuf_v, kb16, q16, k16, rhs16, qd_s, kt_s, gcb_s,
                g16, h_s, qe16, ou_s, gl_s, state_ref, n_heads=n_heads,
                between_phases=lru_branch)


def _mixer(x, mod_l, norm_w, w_in_pad, gate_params, lru_conv_w, lru_conv_b, lru_gate_w,
           lru_gate_b, lru_lam, lru_norm_w, gdn_conv_w, gdn_norm_w, *, layer, ts, n_heads):
    bsz, seq, d = x.shape
    lru_width = lru_conv_w.shape[1]
    hd = GDN_HEAD_DIM
    c = GDN_CHUNK
    gdn_width = n_heads * hd
    n_chunks = ts // c
    n_pad = w_in_pad.shape[2]
    out_width = lru_width + gdn_width
    res = functools.partial(_resident, n_grid_axes=2)
    return pl.pallas_call(
        functools.partial(_mixer_kernel, n_heads=n_heads, lru_width=lru_width),
        grid=(bsz, seq // ts),
        in_specs=[
            pl.BlockSpec((None, ts, d), lambda b, t: (b, t, 0)),
            pl.BlockSpec((None, N_MOD, 1, d), lambda b, t: (b, 0, 0, 0)),
            res((1, d)),
            res((d, n_pad), layer=layer),
            res((SUBLANES, LANES)),
            res((CONV_WIDTH, lru_width)),
            res((1, lru_width)),
            res((lru_width // MXU_DIM, MXU_DIM, 2 * MXU_DIM)),
            res((2, lru_width)),
            res((1, lru_width)),
            res((1, lru_width)),
            res((CONV_WIDTH, 3 * gdn_width)),
            res((1, hd)),
        ],
        out_specs=pl.BlockSpec((None, ts, out_width), lambda b, t: (b, t, 0)),
        out_shape=jax.ShapeDtypeStruct((bsz, seq, out_width), BF16),
        scratch_shapes=[
            pltpu.VMEM((2, SUBLANES, lru_width), F32),
            pltpu.VMEM((1, lru_width), F32),
            pltpu.VMEM((ts, gdn_width), F32),
            pltpu.VMEM((2, SUBLANES, gdn_width), F32),
            pltpu.VMEM((2, SUBLANES, gdn_width), F32),
            pltpu.VMEM((2, SUBLANES, gdn_width), F32),
            pltpu.VMEM((ts, gdn_width), BF16),
            pltpu.VMEM((ts, gdn_width), BF16),
            pltpu.VMEM((ts, gdn_width), BF16),
            pltpu.VMEM((ts, 2 * gdn_width), BF16),
            pltpu.VMEM((ts, gdn_width), F32),
            pltpu.VMEM((ts, gdn_width), F32),
            pltpu.VMEM((ts, gdn_width), F32),
            pltpu.VMEM((n_chunks, n_heads, hd, hd), BF16),
            pltpu.VMEM((n_chunks, n_heads, hd, hd), F32),
            pltpu.VMEM((n_chunks, n_heads, c, hd), BF16),
            pltpu.VMEM((n_chunks, n_heads, c, hd), F32),
            pltpu.VMEM((n_heads, n_chunks, hd), F32),
            pltpu.VMEM((n_heads, hd, hd), F32),
        ],
        compiler_params=_compiler_params(("arbitrary", "arbitrary")),
        name="mixer",
    )(x, mod_l, norm_w, w_in_pad, gate_params, lru_conv_w, lru_conv_b, lru_gate_w,
      lru_gate_b, lru_lam, lru_norm_w, gdn_conv_w, gdn_norm_w)


def _outmlp_kernel(x_ref, mix_ref, mod_ref, wo_ref, nw_ref, wu_ref, wd_ref,
                   fnw_ref, o_ref, *, final_norm, tf):
    mix = jnp.dot(mix_ref[...], wo_ref[...], preferred_element_type=F32)
    x1 = x_ref[...] + mod_ref[2] * mix
    h2 = ((x1 * _rms_scale(x1) * nw_ref[...]) * (1.0 + mod_ref[4]) + mod_ref[3]).astype(BF16)
    dff = wu_ref.shape[1]
    acc = None
    for j in range(dff // tf):
        up = jnp.dot(h2, wu_ref[:, j * tf:(j + 1) * tf], preferred_element_type=F32)
        act = jnp.square(jnp.maximum(up, 0.0)).astype(BF16)
        down = jnp.dot(act, wd_ref[j * tf:(j + 1) * tf, :], preferred_element_type=F32)
        acc = down if acc is None else acc + down
    y = x1 + mod_ref[5] * acc
    if final_norm:
        y = y * _rms_scale(y) * fnw_ref[...]
    o_ref[...] = y


def _outmlp(x2d, mixed2d, mod_l, w_out, norm_w, w_up, w_down, final_w,
            *, layer, seq, tm, tf, final_norm):
    t, d = x2d.shape
    dff = w_up.shape[2]
    tiles_per_seq = seq // tm
    res = functools.partial(_resident, n_grid_axes=1)
    return pl.pallas_call(
        functools.partial(_outmlp_kernel, final_norm=final_norm, tf=tf),
        grid=(t // tm,),
        in_specs=[
            pl.BlockSpec((tm, d), lambda i: (i, 0)),
            pl.BlockSpec((tm, mixed2d.shape[1]), lambda i: (i, 0)),
            pl.BlockSpec((None, N_MOD, 1, d), lambda i: (i // tiles_per_seq, 0, 0, 0)),
            res((d, d), layer=layer),
            res((1, d)),
            res((d, dff), layer=layer),
            res((dff, d), layer=layer),
            res((1, d)),
        ],
        out_specs=pl.BlockSpec((tm, d), lambda i: (i, 0)),
        out_shape=jax.ShapeDtypeStruct((t, d), F32),
        compiler_params=_compiler_params(("arbitrary",)),
        name="outproj_mlp",
    )(x2d, mixed2d, mod_l, w_out, norm_w, w_up, w_down, final_w)


def _block_diag_gates(gate_a, gate_x):
    g, n, _ = gate_a.shape
    per = MXU_DIM // n
    n_blk = g // per

    def diag(w):
        w = w.reshape(n_blk, per, n, n)
        eye = jnp.eye(per, dtype=w.dtype)
        return jnp.einsum("bpij,pq->bpiqj", w, eye).reshape(n_blk, MXU_DIM, MXU_DIM)

    return jnp.concatenate([diag(gate_a), diag(gate_x)], axis=2)


def kernel(x, c, norm_mix_w, norm_mlp_w, w_mod, b_mod, w_in, lru_conv_w, lru_conv_b,
           lru_gate_a_w, lru_gate_a_b, lru_gate_x_w, lru_gate_x_b, lru_lambda, lru_norm_w,
           gdn_conv_w, gdn_a_log, gdn_dt_bias, gdn_norm_w, w_out, w_up, w_down, final_norm_w):
    bsz, seq, d = x.shape
    depth = w_in.shape[0]
    lru_width = lru_conv_w.shape[2]
    n_heads = gdn_a_log.shape[1]
    gdn_width = n_heads * GDN_HEAD_DIM
    in_cols = w_in.shape[2]
    ba_col = 2 * lru_width + 4 * gdn_width
    assert in_cols == ba_col + 2 * n_heads
    n_pad = ba_col + LANES

    mod = _modulation(c, w_mod, b_mod)
    w_in_pad = jnp.pad(w_in.astype(BF16), ((0, 0), (0, 0), (0, n_pad - in_cols)))
    w_out_b = w_out.astype(BF16)
    w_up_b = w_up.astype(BF16)
    w_down_b = w_down.astype(BF16)
    gate_params = jnp.zeros((depth, SUBLANES, LANES), F32)
    gate_params = gate_params.at[:, 0, n_heads:2 * n_heads].set(-jnp.exp(gdn_a_log))
    gate_params = gate_params.at[:, 1, n_heads:2 * n_heads].set(gdn_dt_bias)

    for l in range(depth):
        gate_w = _block_diag_gates(lru_gate_a_w[l], lru_gate_x_w[l]).astype(BF16)
        gate_b = jnp.stack([lru_gate_a_b[l], lru_gate_x_b[l]])
        mixed = _mixer(x, mod[l], norm_mix_w[l][None], w_in_pad, gate_params[l],
                       lru_conv_w[l], lru_conv_b[l][None], gate_w, gate_b,
                       lru_lambda[l][None], lru_norm_w[l][None],
                       gdn_conv_w[l], gdn_norm_w[l][None], layer=l, ts=MIXER_ROWS,
                       n_heads=n_heads)
        x2d = _outmlp(x.reshape(bsz * seq, d), mixed.reshape(bsz * seq, lru_width + gdn_width),
                      mod[l], w_out_b, norm_mlp_w[l][None], w_up_b, w_down_b,
                      final_norm_w[None], layer=l, seq=seq, tm=MLP_ROWS, tf=MLP_FF_CHUNK,
                      final_norm=(l == depth - 1))
        x = x2d.reshape(bsz, seq, d)
    return x
```

```python
import functools
import math

import jax
import jax.numpy as jnp
from jax import lax
from jax.experimental import pallas as pl
from jax.experimental.pallas import tpu as pltpu

LRU_C = 8.0
GDN_HEAD_DIM = 128
GDN_CHUNK = 64
CONV_WIDTH = 4
N_MOD = 6
NORM_EPS = 1e-6

LANES = 128
SUBLANES = 8
MXU_DIM = 256
VMEM_LIMIT_BYTES = 56 * 1024 * 1024

MIXER_ROWS = 512
MLP_ROWS = 1024
MLP_FF_CHUNK = 1024

F32 = jnp.float32
BF16 = jnp.bfloat16


def _compiler_params(semantics):
    return pltpu.CompilerParams(dimension_semantics=semantics,
                                vmem_limit_bytes=VMEM_LIMIT_BYTES)


def _resident(shape, n_grid_axes, layer=None):
    index = (0,) * len(shape)
    if layer is not None:
        shape = (None,) + tuple(shape)
        index = (layer,) + index
    if n_grid_axes == 1:
        index_map = lambda i: index
    else:
        index_map = lambda i, j: index
    return pl.BlockSpec(shape, index_map, pipeline_mode=pl.Buffered(1))


def _sigmoid(x):
    return 0.5 * jnp.tanh(0.5 * x) + 0.5


def _silu(x):
    half = 0.5 * x
    return half * jnp.tanh(half) + half


def _softplus(x):
    return jnp.maximum(x, 0.0) + jnp.log1p(jnp.exp(-jnp.abs(x)))


def _rms_scale(x):
    return lax.rsqrt(jnp.mean(x * x, axis=-1, keepdims=True) + NORM_EPS)


def _l2norm(t, scale=1.0):
    inv = lax.rsqrt(jnp.sum(t * t, axis=-1, keepdims=True) + 1e-6)
    return t * (inv if scale == 1.0 else inv * scale)


def _mod_kernel(c_ref, w_ref, b_ref, o_ref):
    c_act = _silu(c_ref[...]).astype(BF16)
    o_ref[...] = jnp.dot(c_act, w_ref[...].astype(BF16),
                         preferred_element_type=F32) + b_ref[...]


def _modulation(c, w_mod, b_mod):
    depth, d, n = w_mod.shape
    b = c.shape[0]
    bp = max(SUBLANES, b)
    c_pad = jnp.zeros((bp, d), F32).at[:b].set(c)
    tn = n // N_MOD
    out = pl.pallas_call(
        _mod_kernel,
        grid=(depth, n // tn),
        in_specs=[
            pl.BlockSpec((bp, d), lambda l, j: (0, 0)),
            pl.BlockSpec((None, d, tn), lambda l, j: (l, 0, j)),
            pl.BlockSpec((None, 1, tn), lambda l, j: (l, 0, j)),
        ],
        out_specs=pl.BlockSpec((None, bp, tn), lambda l, j: (l, 0, j)),
        out_shape=jax.ShapeDtypeStruct((depth, bp, n), F32),
        compiler_params=_compiler_params(("arbitrary", "arbitrary")),
        name="modulation",
    )(c_pad, w_mod, b_mod.reshape(depth, 1, n))
    return out[:, :b].reshape(depth, b, N_MOD, 1, d)


def _causal_conv(hist_ref, x, w_ref):
    assert w_ref.shape[0] == CONV_WIDTH == 4
    ts = x.shape[0]

    def shifted(hist, val, k):
        cat = jnp.concatenate([hist, val], axis=0)
        return pltpu.roll(cat, k, axis=0)[SUBLANES:]

    x_prev = shifted(hist_ref[0], x, 1)
    v = x * w_ref[1:2, :] + x_prev * w_ref[0:1, :]
    y = x * w_ref[3:4, :] + x_prev * w_ref[2:3, :] + shifted(hist_ref[1], v, 2)
    hist_ref[0] = x[ts - SUBLANES:]
    hist_ref[1] = v[ts - SUBLANES:]
    return y


def _lru_branch(x_lru, y_lru, cw_ref, cb_ref, gw_ref, gb_ref, lam_ref, nw_ref,
                cbuf_ref, hprev_ref):
    ts, width = x_lru.shape
    xr = _causal_conv(cbuf_ref, x_lru, cw_ref) + cb_ref[...]

    xr_b = xr.astype(BF16)
    r_parts, i_parts = [], []
    for blk in range(width // MXU_DIM):
        cols = slice(blk * MXU_DIM, (blk + 1) * MXU_DIM)
        gates = jnp.dot(xr_b[:, cols], gw_ref[blk], preferred_element_type=F32)
        r_parts.append(gates[:, :MXU_DIM])
        i_parts.append(gates[:, MXU_DIM:])
    r = _sigmoid(jnp.concatenate(r_parts, axis=1) + gb_ref[0:1, :])
    i = _sigmoid(jnp.concatenate(i_parts, axis=1) + gb_ref[1:2, :])
    lam = lam_ref[...]
    log_sig_lam = jnp.minimum(lam, 0.0) - jnp.log1p(jnp.exp(-jnp.abs(lam)))
    log_a = LRU_C * r * log_sig_lam
    a = jnp.exp(log_a)
    mult = jnp.sqrt(jnp.maximum(1.0 - a * a, 1e-12))
    b = mult * (i * xr)

    sub = lax.broadcasted_iota(jnp.int32, (ts, width), 0) & (SUBLANES - 1)
    shift = 1
    while shift < SUBLANES:
        valid = sub >= shift
        a_sh = pltpu.roll(a, shift, axis=0)
        b_sh = pltpu.roll(b, shift, axis=0)
        b = b + jnp.where(valid, a * b_sh, 0.0)
        a = jnp.where(valid, a * a_sh, a)
        shift *= 2
    carry = hprev_ref[...]
    groups = []
    for r in range(ts // SUBLANES):
        rows = slice(r * SUBLANES, (r + 1) * SUBLANES)
        h_r = a[rows] * carry + b[rows]
        groups.append(h_r)
        carry = h_r[SUBLANES - 1:SUBLANES, :]
    b = jnp.concatenate(groups, axis=0)
    hprev_ref[...] = carry

    y3 = y_lru * y_lru * y_lru
    gelu = 0.5 * y_lru * (1.0 + jnp.tanh(math.sqrt(2.0 / math.pi) * (y_lru + 0.044715 * y3)))
    m = b * gelu
    return m * _rms_scale(m) * nw_ref[...]


def _gdn_branch(q_pre, k_pre, v_pre, ba, cw_ref, nw_ref, sz_ref, out_ref, out_col,
                cbuf_q, cbuf_k, cbuf_v, kb16, q16, k16, rhs16, qd_s, kt_s, gcb_s,
                g16, h_s, qe16, ou_s, gl_s, state_ref, *, n_heads, between_phases):
    ts = q_pre.shape[0]
    hd = GDN_HEAD_DIM
    c = GDN_CHUNK
    n_chunks = ts // c
    width = n_heads * hd

    ri = lax.broadcasted_iota(jnp.int32, (c, c), 0)
    ci = lax.broadcasted_iota(jnp.int32, (c, c), 1)
    tril = (ri >= ci).astype(F32)
    causal = ri >= ci
    strict = ri > ci
    eye = (ri == ci).astype(F32)
    pair_mask = (ri == ci + 1) & ((ri & 1) == 1)
    merge_masks = []
    sz = 2
    while sz < c:
        sh = sz.bit_length() - 1
        rb = ri >> sh
        merge_masks.append((rb == (ci >> sh) + 1) & ((rb & 1) == 1))
        sz *= 2

    qc = _silu(_causal_conv(cbuf_q, q_pre, cw_ref.at[:, 0:width]))
    kc = _silu(_causal_conv(cbuf_k, k_pre, cw_ref.at[:, width:2 * width]))
    vc = _silu(_causal_conv(cbuf_v, v_pre, cw_ref.at[:, 2 * width:3 * width]))
    gc_all = jnp.concatenate(
        [jnp.dot(tril, ba[ic * c:(ic + 1) * c], precision=lax.Precision.HIGHEST,
                 preferred_element_type=F32) for ic in range(n_chunks)], axis=0)
    for h in range(n_heads):
        cols = slice(h * hd, (h + 1) * hd)
        beta = jnp.broadcast_to(ba[:, h:h + 1], (ts, hd))
        gcb = jnp.broadcast_to(gc_all[:, n_heads + h:n_heads + h + 1], (ts, hd))
        gcb3 = gcb.reshape(n_chunks, c, hd)
        g_last = gcb3[:, c - 1:c, :]
        eg = jnp.exp(gcb)
        kt_scale = jnp.exp(g_last - gcb3).reshape(ts, hd)
        qn = _l2norm(qc[:, cols], scale=hd ** -0.5)
        kn = _l2norm(kc[:, cols])
        kb = kn * beta
        kb16[:, cols] = kb.astype(BF16)
        q16[:, cols] = qn.astype(BF16)
        k16[:, cols] = kn.astype(BF16)
        rhs16[:, 2 * h * hd:(2 * h + 1) * hd] = (vc[:, cols] * beta).astype(BF16)
        rhs16[:, (2 * h + 1) * hd:(2 * h + 2) * hd] = (kb * eg).astype(BF16)
        qd_s[:, cols] = qn * eg
        kt_s[:, cols] = kn * kt_scale
        gcb_s[:, cols] = gcb
        gl_s[h] = jnp.exp(g_last).reshape(n_chunks, hd)

    items = [(ic, h) for ic in range(n_chunks) for h in range(n_heads)]
    rows_of = [slice(ic * c, (ic + 1) * c) for ic, _ in items]
    cols_of = [slice(h * hd, (h + 1) * hd) for _, h in items]
    n_items = len(items)
    lmats, attns, ps = [], [], []
    for i in range(n_items):
        rows, cols = rows_of[i], cols_of[i]
        gcb = gcb_s[rows, cols]
        diff = gcb[:, 0:c] - gcb.T[0:c, :]
        decay = jnp.exp(jnp.where(causal, diff, 0.0))
        a_lhs = jnp.concatenate([kb16[rows, cols], q16[rows, cols]], axis=0)
        kkqk = lax.dot_general(a_lhs, k16[rows, cols], (((1,), (1,)), ((), ())),
                               preferred_element_type=F32)
        lmat = jnp.where(strict, kkqk[0:c] * decay, 0.0)
        lmats.append(lmat)
        attns.append(jnp.where(causal, kkqk[c:2 * c] * decay, 0.0).astype(BF16))
        ps.append(eye - jnp.where(pair_mask, lmat, 0.0))
    l_bs = [lmat.astype(BF16) for lmat in lmats]
    for merge_mask in merge_masks:
        p_bs = [p.astype(BF16) for p in ps]
        cps = [jnp.where(merge_mask,
                         jnp.dot(l_bs[i], p_bs[i], preferred_element_type=F32), 0.0)
               for i in range(n_items)]
        ps = [ps[i] - jnp.dot(p_bs[i], cps[i].astype(BF16), preferred_element_type=F32)
              for i in range(n_items)]
    wus = []
    for i in range(n_items):
        h = items[i][1]
        uw = jnp.dot(ps[i].astype(BF16), rhs16[rows_of[i], 2 * h * hd:(2 * h + 2) * hd],
                     preferred_element_type=F32)
        wus.append(jnp.concatenate([uw[:, hd:2 * hd], uw[:, 0:hd]], axis=1).astype(BF16))
    for i in range(n_items):
        ic, h = items[i]
        rows, cols = rows_of[i], cols_of[i]
        lhs = jnp.concatenate([kt_s[rows, cols].T.astype(BF16), attns[i]], axis=0)
        prod = jnp.dot(lhs, wus[i], preferred_element_type=F32)
        g16[ic, h] = (-prod[0:hd, 0:hd]).astype(BF16)
        h_s[ic, h] = prod[0:hd, hd:2 * hd]
        qe16[ic, h] = (qd_s[rows, cols] - prod[hd:hd + c, 0:hd]).astype(BF16)
        ou_s[ic, h] = prod[hd:hd + c, hd:2 * hd]

    between_phases()

    nw = nw_ref[...]
    states = [state_ref[h] for h in range(n_heads)]
    for ic in range(n_chunks):
        entry_states = []
        for h in range(n_heads):
            s_b = states[h].astype(BF16)
            entry_states.append(s_b)
            states[h] = (states[h] * gl_s[h, ic:ic + 1, :]
                         + jnp.dot(g16[ic, h], s_b, preferred_element_type=F32)
                         + h_s[ic, h])
        rows = slice(ic * c, (ic + 1) * c)
        for h in range(n_heads):
            o = jnp.dot(qe16[ic, h], entry_states[h], preferred_element_type=F32) + ou_s[ic, h]
            ocols = slice(out_col + h * hd, out_col + (h + 1) * hd)
            out_ref[rows, ocols] = (o * _rms_scale(o) * nw
                                    * sz_ref[rows, h * hd:(h + 1) * hd]).astype(out_ref.dtype)
    for h in range(n_heads):
        state_ref[h] = states[h]


def _mixer_kernel(x_ref, mod_ref, nw_ref, w_ref, gp_ref,
                  lcw_ref, lcb_ref, lgw_ref, lgb_ref, lam_ref, lnw_ref, gcw_ref, gnw_ref,
                  o_ref,
                  lru_cbuf, hprev_ref, sz_ref, cbuf_q, cbuf_k, cbuf_v, kb16, q16, k16, rhs16,
                  qd_s, kt_s, gcb_s, g16, h_s, qe16, ou_s, gl_s, state_ref,
                  *, n_heads, lru_width):
    gdn_width = n_heads * GDN_HEAD_DIM
    q_col = 2 * lru_width
    z_col = q_col + 3 * gdn_width
    ba_col = z_col + gdn_width

    @pl.when(pl.program_id(1) == 0)
    def _():
        for hist in (lru_cbuf, cbuf_q, cbuf_k, cbuf_v):
            hist[...] = jnp.zeros_like(hist)
        hprev_ref[...] = jnp.zeros_like(hprev_ref)
        state_ref[...] = jnp.zeros_like(state_ref)

    x = x_ref[...]
    gain = nw_ref[...] * (1.0 + mod_ref[1])
    h = ((x * _rms_scale(x)) * gain + mod_ref[0]).astype(BF16)

    def proj(lo, hi):
        return jnp.dot(h, w_ref[:, lo:hi], preferred_element_type=F32)

    x_lru = proj(0, lru_width)
    y_lru = proj(lru_width, q_col)
    q_pre = proj(q_col, q_col + gdn_width)
    k_pre = proj(q_col + gdn_width, q_col + 2 * gdn_width)
    v_pre = proj(q_col + 2 * gdn_width, z_col)
    sz_ref[...] = _silu(proj(z_col, ba_col))
    tail = proj(ba_col, ba_col + LANES)
    lane = lax.broadcasted_iota(jnp.int32, tail.shape, 1)
    g = gp_ref[0:1, :] * _softplus(tail + gp_ref[1:2, :])
    ba = jnp.where(lane < n_heads, _sigmoid(tail), jnp.where(lane < 2 * n_heads, g, 0.0))

    def lru_branch():
        o_ref[:, 0:lru_width] = _lru_branch(
            x_lru, y_lru, lcw_ref, lcb_ref, lgw_ref, lgb_ref, lam_ref, lnw_ref,
            lru_cbuf, hprev_ref).astype(o_ref.dtype)

    _gdn_branch(q_pre, k_pre, v_pre, ba, gcw_ref, gnw_ref, sz_ref, o_ref, lru_width,
                cbuf_q, cbuf_k, cbuf_v, kb16, q16, k16, rhs16, qd_s, kt_s, gcb_s,
                g16, h_s, qe16, ou_s, gl_s, state_ref, n_heads=n_heads,
                between_phases=lru_branch)


def _mixer(x, mod_l, norm_w, w_in_pad, gate_params, lru_conv_w, lru_conv_b, lru_gate_w,
           lru_gate_b, lru_lam, lru_norm_w, gdn_conv_w, gdn_norm_w, *, layer, ts, n_heads):
    bsz, seq, d = x.shape
    lru_width = lru_conv_w.shape[1]
    hd = GDN_HEAD_DIM
    c = GDN_CHUNK
    gdn_width = n_heads * hd
    n_chunks = ts // c
    n_pad = w_in_pad.shape[2]
    out_width = lru_width + gdn_width
    res = functools.partial(_resident, n_grid_axes=2)
    return pl.pallas_call(
        functools.partial(_mixer_kernel, n_heads=n_heads, lru_width=lru_width),
        grid=(bsz, seq // ts),
        in_specs=[
            pl.BlockSpec((None, ts, d), lambda b, t: (b, t, 0)),
            pl.BlockSpec((None, N_MOD, 1, d), lambda b, t: (b, 0, 0, 0)),
            res((1, d)),
            res((d, n_pad), layer=layer),
            res((SUBLANES, LANES)),
            res((CONV_WIDTH, lru_width)),
            res((1, lru_width)),
            res((lru_width // MXU_DIM, MXU_DIM, 2 * MXU_DIM)),
            res((2, lru_width)),
            res((1, lru_width)),
            res((1, lru_width)),
            res((CONV_WIDTH, 3 * gdn_width)),
            res((1, hd)),
        ],
        out_specs=pl.BlockSpec((None, ts, out_width), lambda b, t: (b, t, 0)),
        out_shape=jax.ShapeDtypeStruct((bsz, seq, out_width), BF16),
        scratch_shapes=[
            pltpu.VMEM((2, SUBLANES, lru_width), F32),
            pltpu.VMEM((1, lru_width), F32),
            pltpu.VMEM((ts, gdn_width), F32),
            pltpu.VMEM((2, SUBLANES, gdn_width), F32),
            pltpu.VMEM((2, SUBLANES, gdn_width), F32),
            pltpu.VMEM((2, SUBLANES, gdn_width), F32),
            pltpu.VMEM((ts, gdn_width), BF16),
            pltpu.VMEM((ts, gdn_width), BF16),
            pltpu.VMEM((ts, gdn_width), BF16),
            pltpu.VMEM((ts, 2 * gdn_width), BF16),
            pltpu.VMEM((ts, gdn_width), F32),
            pltpu.VMEM((ts, gdn_width), F32),
            pltpu.VMEM((ts, gdn_width), F32),
            pltpu.VMEM((n_chunks, n_heads, hd, hd), BF16),
            pltpu.VMEM((n_chunks, n_heads, hd, hd), F32),
            pltpu.VMEM((n_chunks, n_heads, c, hd), BF16),
            pltpu.VMEM((n_chunks, n_heads, c, hd), F32),
            pltpu.VMEM((n_heads, n_chunks, hd), F32),
            pltpu.VMEM((n_heads, hd, hd), F32),
        ],
        compiler_params=_compiler_params(("arbitrary", "arbitrary")),
        name="mixer",
    )(x, mod_l, norm_w, w_in_pad, gate_params, lru_conv_w, lru_conv_b, lru_gate_w,
      lru_gate_b, lru_lam, lru_norm_w, gdn_conv_w, gdn_norm_w)


def _outmlp_kernel(x_ref, mix_ref, mod_ref, wo_ref, nw_ref, wu_ref, wd_ref,
                   fnw_ref, o_ref, *, final_norm, tf):
    mix = jnp.dot(mix_ref[...], wo_ref[...], preferred_element_type=F32)
    x1 = x_ref[...] + mod_ref[2] * mix
    h2 = ((x1 * _rms_scale(x1) * nw_ref[...]) * (1.0 + mod_ref[4]) + mod_ref[3]).astype(BF16)
    dff = wu_ref.shape[1]
    acc = None
    for j in range(dff // tf):
        up = jnp.dot(h2, wu_ref[:, j * tf:(j + 1) * tf], preferred_element_type=F32)
        act = jnp.square(jnp.maximum(up, 0.0)).astype(BF16)
        down = jnp.dot(act, wd_ref[j * tf:(j + 1) * tf, :], preferred_element_type=F32)
        acc = down if acc is None else acc + down
    y = x1 + mod_ref[5] * acc
    if final_norm:
        y = y * _rms_scale(y) * fnw_ref[...]
    o_ref[...] = y


def _outmlp(x2d, mixed2d, mod_l, w_out, norm_w, w_up, w_down, final_w,
            *, layer, seq, tm, tf, final_norm):
    t, d = x2d.shape
    dff = w_up.shape[2]
    tiles_per_seq = seq // tm
    res = functools.partial(_resident, n_grid_axes=1)
    return pl.pallas_call(
        functools.partial(_outmlp_kernel, final_norm=final_norm, tf=tf),
        grid=(t // tm,),
        in_specs=[
            pl.BlockSpec((tm, d), lambda i: (i, 0)),
            pl.BlockSpec((tm, mixed2d.shape[1]), lambda i: (i, 0)),
            pl.BlockSpec((None, N_MOD, 1, d), lambda i: (i // tiles_per_seq, 0, 0, 0)),
            res((d, d), layer=layer),
            res((1, d)),
            res((d, dff), layer=layer),
            res((dff, d), layer=layer),
            res((1, d)),
        ],
        out_specs=pl.BlockSpec((tm, d), lambda i: (i, 0)),
        out_shape=jax.ShapeDtypeStruct((t, d), F32),
        compiler_params=_compiler_params(("arbitrary",)),
        name="outproj_mlp",
    )(x2d, mixed2d, mod_l, w_out, norm_w, w_up, w_down, final_w)


def _block_diag_gates(gate_a, gate_x):
    g, n, _ = gate_a.shape
    per = MXU_DIM // n
    n_blk = g // per

    def diag(w):
        w = w.reshape(n_blk, per, n, n)
        eye = jnp.eye(per, dtype=w.dtype)
        return jnp.einsum("bpij,pq->bpiqj", w, eye).reshape(n_blk, MXU_DIM, MXU_DIM)

    return jnp.concatenate([diag(gate_a), diag(gate_x)], axis=2)


def kernel(x, c, norm_mix_w, norm_mlp_w, w_mod, b_mod, w_in, lru_conv_w, lru_conv_b,
           lru_gate_a_w, lru_gate_a_b, lru_gate_x_w, lru_gate_x_b, lru_lambda, lru_norm_w,
           gdn_conv_w, gdn_a_log, gdn_dt_bias, gdn_norm_w, w_out, w_up, w_down, final_norm_w):
    bsz, seq, d = x.shape
    depth = w_in.shape[0]
    lru_width = lru_conv_w.shape[2]
    n_heads = gdn_a_log.shape[1]
    gdn_width = n_heads * GDN_HEAD_DIM
    in_cols = w_in.shape[2]
    ba_col = 2 * lru_width + 4 * gdn_width
    assert in_cols == ba_col + 2 * n_heads
    n_pad = ba_col + LANES

    mod = _modulation(c, w_mod, b_mod)
    w_in_pad = jnp.pad(w_in.astype(BF16), ((0, 0), (0, 0), (0, n_pad - in_cols)))
    w_out_b = w_out.astype(BF16)
    w_up_b = w_up.astype(BF16)
    w_down_b = w_down.astype(BF16)
    gate_params = jnp.zeros((depth, SUBLANES, LANES), F32)
    gate_params = gate_params.at[:, 0, n_heads:2 * n_heads].set(-jnp.exp(gdn_a_log))
    gate_params = gate_params.at[:, 1, n_heads:2 * n_heads].set(gdn_dt_bias)

    for l in range(depth):
        gate_w = _block_diag_gates(lru_gate_a_w[l], lru_gate_x_w[l]).astype(BF16)
        gate_b = jnp.stack([lru_gate_a_b[l], lru_gate_x_b[l]])
        mixed = _mixer(x, mod[l], norm_mix_w[l][None], w_in_pad, gate_params[l],
                       lru_conv_w[l], lru_conv_b[l][None], gate_w, gate_b,
                       lru_lambda[l][None], lru_norm_w[l][None],
                       gdn_conv_w[l], gdn_norm_w[l][None], layer=l, ts=MIXER_ROWS,
                       n_heads=n_heads)
        x2d = _outmlp(x.reshape(bsz * seq, d), mixed.reshape(bsz * seq, lru_width + gdn_width),
                      mod[l], w_out_b, norm_mlp_w[l][None], w_up_b, w_down_b,
                      final_norm_w[None], layer=l, seq=seq, tm=MLP_ROWS, tf=MLP_FF_CHUNK,
                      final_norm=(l == depth - 1))
        x = x2d.reshape(bsz, seq, d)
    return x
```
